```python
import math
import jax, jax.numpy as jnp
from jax import lax
import numpy as np

D_MODEL = 2048
BATCH = 8
SEQ = 4096
DEPTH = 1
DEC_BATCH = 16
DEC_SEQ = 32
PAST_LEN = 4096

CHUNK = 64
MIX_WIDTH = D_MODEL
FOX_WIDTH = MIX_WIDTH // 2
FOX_HEADS = 8
FOX_HEAD_DIM = FOX_WIDTH // FOX_HEADS
GMLP_WIDTH = MIX_WIDTH - FOX_WIDTH
GMLP_GROUPS = 4
GMLP_GROUP_DIM = GMLP_WIDTH // GMLP_GROUPS
GMLP_CHUNK = 2 * CHUNK
Q_BLOCK = 128
N_MEM = 256
MEM_HEADS = 4
MEM_HEAD_DIM = D_MODEL // MEM_HEADS
D_FF = 4 * D_MODEL
ALPHA = (2 * DEPTH) ** 0.25
BETA = (8 * DEPTH) ** -0.25
LN_EPS = 1e-5
IN_COLS = 3 * FOX_WIDTH + FOX_HEADS + 2 * GMLP_WIDTH
FOX_SCALE = FOX_HEAD_DIM ** -0.5
MEM_SCALE = MEM_HEAD_DIM ** -0.5

kernel_name = "fox_gmlp_hybrid_stream_step"


def layer_norm(x, g, b):
    xf = x.astype(jnp.float32)
    mu = jnp.mean(xf, axis=-1, keepdims=True)
    var = jnp.mean(jnp.square(xf - mu), axis=-1, keepdims=True)
    return ((xf - mu) * lax.rsqrt(var + LN_EPS) * g.astype(jnp.float32) + b.astype(jnp.float32)).astype(x.dtype)


def project_in(x, w_in, b_f, sgu_g, sgu_b):
    B, T, _ = x.shape
    z = x @ w_in
    o1, o2, o3 = FOX_WIDTH, 2 * FOX_WIDTH, 3 * FOX_WIDTH
    o4 = o3 + FOX_HEADS
    o5 = o4 + GMLP_WIDTH
    q = z[..., :o1].reshape(B, T, FOX_HEADS, FOX_HEAD_DIM)
    k = z[..., o1:o2].reshape(B, T, FOX_HEADS, FOX_HEAD_DIM)
    v = z[..., o2:o3].reshape(B, T, FOX_HEADS, FOX_HEAD_DIM)
    logf = jax.nn.log_sigmoid((z[..., o3:o4] + b_f).astype(jnp.float32))
    u = jax.nn.gelu(z[..., o4:o5])
    g = layer_norm(jax.nn.gelu(z[..., o5:]), sgu_g, sgu_b)
    return q, k, v, logf, u, g


def fox_block(q, k, v, c_q, c_k, q_pos, k_pos):
    s = jnp.einsum('bqhd,bkhd->bhqk', q, k).astype(jnp.float32) * FOX_SCALE
    s = s + (jnp.transpose(c_q, (0, 2, 1))[..., :, None] - jnp.transpose(c_k, (0, 2, 1))[..., None, :])
    mask = k_pos[None, :] <= q_pos[:, None]
    s = jnp.where(mask, s, -jnp.inf)
    p = jax.nn.softmax(s, axis=-1)
    return jnp.einsum('bhqk,bkhd->bqhd', p.astype(v.dtype), v)


def fox_prompt(q, k, v, logf):
    S = q.shape[1]
    c = jnp.cumsum(logf, axis=1)
    outs = []
    for i in range(S // Q_BLOCK):
        lo, hi = i * Q_BLOCK, (i + 1) * Q_BLOCK
        outs.append(fox_block(q[:, lo:hi], k[:, :hi], v[:, :hi], c[:, lo:hi], c[:, :hi],
                              jnp.arange(lo, hi), jnp.arange(hi)))
    return jnp.concatenate(outs, axis=1)


def fox_sample(q, k, v, logf, cache_k, cache_v, cache_logf):
    P, T = cache_k.shape[1], q.shape[1]
    k_all = jnp.concatenate([cache_k, k.astype(cache_k.dtype)], axis=1)
    v_all = jnp.concatenate([cache_v, v.astype(cache_v.dtype)], axis=1)
    c = jnp.cumsum(jnp.concatenate([cache_logf.astype(jnp.float32), logf], axis=1), axis=1)
    return fox_block(q, k_all, v_all, c[:, P:], c, jnp.arange(P, P + T), jnp.arange(P + T))


def sgu_prompt(u, g, w_s, b_s):
    B, S, _ = u.shape
    n = S // GMLP_CHUNK
    ws = w_s * jnp.tril(jnp.ones((GMLP_CHUNK, GMLP_CHUNK), w_s.dtype))
    gr = g.reshape(B, n, GMLP_CHUNK, GMLP_GROUPS, GMLP_GROUP_DIM)
    s = jnp.einsum('gts,bnsgc->bntgc', ws, gr) + jnp.transpose(b_s)[None, None, :, :, None]
    return u * s.reshape(B, S, GMLP_WIDTH)


def sgu_sample(u, g, w_s, b_s):
    B, T, _ = u.shape
    ws = (w_s * jnp.tril(jnp.ones((GMLP_CHUNK, GMLP_CHUNK), w_s.dtype)))[:, :T, :T]
    gr = g.reshape(B, T, GMLP_GROUPS, GMLP_GROUP_DIM)
    s = jnp.einsum('gts,bsgc->btgc', ws, gr) + jnp.transpose(b_s[:, :T])[None, :, :, None]
    return u * s.reshape(B, T, GMLP_WIDTH)


def mem_kv(mem, w_mk, w_mv):
    B, M, _ = mem.shape
    return ((mem @ w_mk).reshape(B, M, MEM_HEADS, MEM_HEAD_DIM),
            (mem @ w_mv).reshape(B, M, MEM_HEADS, MEM_HEAD_DIM))


def mem_attend(x, mk, mv, w_mq, w_mo):
    B, T, _ = x.shape
    q = (x @ w_mq).reshape(B, T, MEM_HEADS, MEM_HEAD_DIM)
    s = jnp.einsum('bqhd,bkhd->bhqk', q, mk.astype(q.dtype)).astype(jnp.float32) * MEM_SCALE
    p = jax.nn.softmax(s, axis=-1)
    o = jnp.einsum('bhqk,bkhd->bqhd', p.astype(q.dtype), mv.astype(q.dtype))
    return o.reshape(B, T, D_MODEL) @ w_mo


def ffn(x, w_up, w_down):
    return jnp.square(jax.nn.relu(x @ w_up)) @ w_down


def post_sublayers(h, mix, mk, mv, l, ln1_g, ln1_b, w_mq, w_mo, ln2_g, ln2_b, w_up, w_down, ln3_g, ln3_b):
    h = layer_norm(ALPHA * h + mix, ln1_g[l], ln1_b[l])
    h = layer_norm(ALPHA * h + mem_attend(h, mk, mv, w_mq[l], w_mo[l]), ln2_g[l], ln2_b[l])
    return layer_norm(ALPHA * h + ffn(h, w_up[l], w_down[l]), ln3_g[l], ln3_b[l])


def setup_inputs(seed: int = 0) -> dict:
    key = jax.random.key(seed)
    ks = iter(jax.random.split(key, 40))
    nrm = lambda shape, s=1.0: jax.random.normal(next(ks), shape, jnp.float32) * s
    L = DEPTH
    return {
        "x_prompt": nrm((BATCH, SEQ, D_MODEL)),
        "x_sample": nrm((DEC_BATCH, DEC_SEQ, D_MODEL)),
        "mem_prompt": nrm((BATCH, N_MEM, D_MODEL)),
        "cache_fox_k": nrm((L, DEC_BATCH, PAST_LEN, FOX_HEADS, FOX_HEAD_DIM)),
        "cache_fox_v": nrm((L, DEC_BATCH, PAST_LEN, FOX_HEADS, FOX_HEAD_DIM)),
        "cache_fox_logf": jax.nn.log_sigmoid(2.0 + nrm((L, DEC_BATCH, PAST_LEN, FOX_HEADS))),
        "cache_mem_k": nrm((L, DEC_BATCH, N_MEM, MEM_HEADS, MEM_HEAD_DIM)),
        "cache_mem_v": nrm((L, DEC_BATCH, N_MEM, MEM_HEADS, MEM_HEAD_DIM)),
        "w_in": nrm((L, D_MODEL, IN_COLS), D_MODEL ** -0.5),
        "b_f": 2.0 + nrm((L, FOX_HEADS), 0.1),
        "sgu_ln_g": 1.0 + nrm((L, GMLP_WIDTH), 0.01),
        "sgu_ln_b": nrm((L, GMLP_WIDTH), 0.01),
        "w_s": nrm((L, GMLP_GROUPS, GMLP_CHUNK, GMLP_CHUNK), GMLP_CHUNK ** -0.5),
        "b_s": 1.0 + nrm((L, GMLP_GROUPS, GMLP_CHUNK), 0.01),
        "w_out": nrm((L, MIX_WIDTH, D_MODEL), BETA * MIX_WIDTH ** -0.5),
        "ln1_g": 1.0 + nrm((L, D_MODEL), 0.01),
        "ln1_b": nrm((L, D_MODEL), 0.01),
        "w_mq": nrm((L, D_MODEL, D_MODEL), D_MODEL ** -0.5),
        "w_mk": nrm((L, D_MODEL, D_MODEL), D_MODEL ** -0.5),
        "w_mv": nrm((L, D_MODEL, D_MODEL), D_MODEL ** -0.5),
        "w_mo": nrm((L, D_MODEL, D_MODEL), BETA * D_MODEL ** -0.5),
        "ln2_g": 1.0 + nrm((L, D_MODEL), 0.01),
        "ln2_b": nrm((L, D_MODEL), 0.01),
        "w_up": nrm((L, D_MODEL, D_FF), D_MODEL ** -0.5),
        "w_down": nrm((L, D_FF, D_MODEL), BETA * D_FF ** -0.5),
        "ln3_g": 1.0 + nrm((L, D_MODEL), 0.01),
        "ln3_b": nrm((L, D_MODEL), 0.01),
    }


def reference(x_prompt, x_sample, mem_prompt, cache_fox_k, cache_fox_v, cache_fox_logf,
              cache_mem_k, cache_mem_v, w_in, b_f, sgu_ln_g, sgu_ln_b, w_s, b_s, w_out,
              ln1_g, ln1_b, w_mq, w_mk, w_mv, w_mo, ln2_g, ln2_b, w_up, w_down, ln3_g, ln3_b):
    hp, hs = x_prompt, x_sample
    fkp, fvp, flp, mkp, mvp = [], [], [], [], []
    fks, fvs, fls, gvs = [], [], [], []
    for l in range(DEPTH):
        B, S, _ = hp.shape
        q, k, v, logf, u, g = project_in(hp, w_in[l], b_f[l], sgu_ln_g[l], sgu_ln_b[l])
        fo = fox_prompt(q, k, v, logf).reshape(B, S, FOX_WIDTH)
        go = sgu_prompt(u, g, w_s[l], b_s[l])
        mix = jnp.concatenate([fo, go], axis=-1) @ w_out[l]
        mk, mv = mem_kv(mem_prompt, w_mk[l], w_mv[l])
        fkp.append(k); fvp.append(v); flp.append(logf); mkp.append(mk); mvp.append(mv)
        hp = post_sublayers(hp, mix, mk, mv, l, ln1_g, ln1_b, w_mq, w_mo, ln2_g, ln2_b,
                            w_up, w_down, ln3_g, ln3_b)
        Bs, T, _ = hs.shape
        q, k, v, logf, u, g = project_in(hs, w_in[l], b_f[l], sgu_ln_g[l], sgu_ln_b[l])
        fo = fox_sample(q, k, v, logf, cache_fox_k[l], cache_fox_v[l], cache_fox_logf[l]).reshape(Bs, T, FOX_WIDTH)
        go = sgu_sample(u, g, w_s[l], b_s[l])
        mix = jnp.concatenate([fo, go], axis=-1) @ w_out[l]
        fks.append(k); fvs.append(v); fls.append(logf); gvs.append(g)
        hs = post_sublayers(hs, mix, cache_mem_k[l], cache_mem_v[l], l, ln1_g, ln1_b, w_mq, w_mo,
                            ln2_g, ln2_b, w_up, w_down, ln3_g, ln3_b)
    return (hp, hs, jnp.stack(fkp), jnp.stack(fvp), jnp.stack(flp), jnp.stack(mkp), jnp.stack(mvp),
            jnp.stack(fks), jnp.stack(fvs), jnp.stack(fls), jnp.stack(gvs))
```

```python
import functools
import math

import jax
import jax.numpy as jnp
from jax import lax
from jax.experimental import pallas as pl
from jax.experimental.pallas import tpu as pltpu

F32 = jnp.float32
BF16 = jnp.bfloat16

LN_EPS = 1e-5
LANES = 128
GELU_C = math.sqrt(2.0 / math.pi)
VMEM_LIMIT = 56 * 1024 * 1024


def _dot(a, b):
    return jnp.dot(a, b, preferred_element_type=F32)


def _dot_nt(a, b):
    return lax.dot_general(a, b, (((1,), (1,)), ((), ())), preferred_element_type=F32)


def _gelu(x):
    return 0.5 * x * (1.0 + jnp.tanh(GELU_C * (x + 0.044715 * (x * x * x))))


def _layer_norm(x, g, b):
    mu = jnp.mean(x, axis=-1, keepdims=True)
    xc = x - mu
    var = jnp.mean(xc * xc, axis=-1, keepdims=True)
    return xc * lax.rsqrt(var + LN_EPS) * g + b


def _log_sigmoid(x):
    return jnp.minimum(x, 0.0) - jnp.log1p(jnp.exp(-jnp.abs(x)))


def _resident(shape):
    nd = len(shape)
    return pl.BlockSpec(shape, lambda *_: (0,) * nd, pipeline_mode=pl.Buffered(1))


def _params(*sem):
    return pltpu.CompilerParams(dimension_semantics=sem, vmem_limit_bytes=VMEM_LIMIT)


def _in_proj_kernel(x_ref, wqkv_ref, wf_ref, bf_ref, wug_ref, lg_ref, lb_ref, ws_ref, bs_ref,
                    q_ref, k_ref, v_ref, logf_ref, go_ref, *g_ref,
                    fox_w, n_heads, gw, n_groups, period, q_scale):
    tm = x_ref.shape[0]
    chunk = ws_ref.shape[1]
    gd = gw // n_groups
    xb = x_ref[...].astype(BF16)
    q_ref[...] = (_dot(xb, wqkv_ref[:, 0:fox_w]) * q_scale).astype(BF16)
    k_ref[...] = _dot(xb, wqkv_ref[:, fox_w:2 * fox_w])
    v_ref[...] = _dot(xb, wqkv_ref[:, 2 * fox_w:3 * fox_w])
    zf = _dot(xb, wf_ref[...]) + bf_ref[...]
    logf_ref[...] = _log_sigmoid(zf)[:, :n_heads]
    g = _layer_norm(_gelu(_dot(xb, wug_ref[:, gw:2 * gw])), lg_ref[...], lb_ref[...])
    if g_ref:
        g_ref[0][...] = g
    gb = g.astype(BF16)
    u = _gelu(_dot(xb, wug_ref[:, 0:gw]))
    r = lax.broadcasted_iota(jnp.int32, (chunk, chunk), 0)
    c = lax.broadcasted_iota(jnp.int32, (chunk, chunk), 1)
    sh = period.bit_length() - 1
    keep = ((r >> sh) == (c >> sh)) & ((c & (period - 1)) <= (r & (period - 1)))
    for gi in range(n_groups):
        wsg = jnp.where(keep, ws_ref[gi], 0.0).astype(BF16)
        cols = slice(gi * gd, (gi + 1) * gd)
        for ci in range(tm // chunk):
            rows = slice(ci * chunk, (ci + 1) * chunk)
            s = _dot(wsg, gb[rows, cols]) + bs_ref[gi]
            go_ref[rows, cols] = (u[rows, cols] * s).astype(BF16)


def _in_proj(x, wqkv, wf, bf, wug, lg, lb, ws, bs, *, tm, period, n_heads, q_scale, emit_g):
    m, d = x.shape
    fox_w = wqkv.shape[1] // 3
    gw = wug.shape[1] // 2
    n_groups = ws.shape[0]
    row = lambda w: pl.BlockSpec((tm, w), lambda i: (i, 0))
    out_shape = [jax.ShapeDtypeStruct((m, fox_w), BF16),
                 jax.ShapeDtypeStruct((m, fox_w), F32),
                 jax.ShapeDtypeStruct((m, fox_w), F32),
                 jax.ShapeDtypeStruct((m, n_heads), F32),
                 jax.ShapeDtypeStruct((m, gw), BF16)]
    out_specs = [row(fox_w), row(fox_w), row(fox_w), row(n_heads), row(gw)]
    if emit_g:
        out_shape.append(jax.ShapeDtypeStruct((m, gw), F32))
        out_specs.append(row(gw))
    kern = functools.partial(_in_proj_kernel, fox_w=fox_w, n_heads=n_heads, gw=gw,
                             n_groups=n_groups, period=period, q_scale=q_scale)
    return pl.pallas_call(
        kern, grid=(m // tm,),
        in_specs=[row(d), _resident(wqkv.shape), _resident(wf.shape), _resident(bf.shape),
                  _resident(wug.shape), _resident(lg.shape), _resident(lb.shape),
                  _resident(ws.shape), _resident(bs.shape)],
        out_specs=out_specs, out_shape=out_shape,
        compiler_params=_params("parallel"), name="in_proj_sgu",
    )(x, wqkv, wf, bf, wug, lg, lb, ws, bs)


def _cumsum_kernel(x_ref, o_ref):
    x = x_ref[0]
    rows = x.shape[0]
    hi = lax.Precision.HIGHEST
    ii = lax.broadcasted_iota(jnp.int32, (LANES, LANES), 0)
    jj = lax.broadcasted_iota(jnp.int32, (LANES, LANES), 1)
    within = jnp.dot(x, (ii <= jj).astype(F32), precision=hi, preferred_element_type=F32)
    totals = jnp.dot(x, jnp.ones((LANES, LANES), F32), precision=hi, preferred_element_type=F32)
    ri = lax.broadcasted_iota(jnp.int32, (rows, rows), 0)
    rj = lax.broadcasted_iota(jnp.int32, (rows, rows), 1)
    before = jnp.dot((rj < ri).astype(F32), totals, precision=hi, preferred_element_type=F32)
    o_ref[0] = within + before


def _cumsum_rows(x):
    n, rows, _ = x.shape
    spec = pl.BlockSpec((1, rows, LANES), lambda i: (i, 0, 0))
    return pl.pallas_call(
        _cumsum_kernel, grid=(n,), in_specs=[spec], out_specs=spec,
        out_shape=jax.ShapeDtypeStruct(x.shape, F32),
        compiler_params=_params("parallel"), name="logf_cumsum",
    )(x)


def _fox_prompt_kernel(q_ref, k_ref, v_ref, c_ref, o_ref, kt_ref, vb_ref, *, blk):
    qi = pl.program_id(2)
    seq, dh = k_ref.shape[1], k_ref.shape[2]

    @pl.when(qi == 0)
    def _():
        for r0 in range(0, seq, 512):
            kt_ref[:, r0:r0 + 512] = k_ref[0, r0:r0 + 512, :].T.astype(BF16)
            vb_ref[r0:r0 + 512, :] = v_ref[0, r0:r0 + 512, :].astype(BF16)

    q = q_ref[0]

    def scores(j):
        start = pl.multiple_of(j * blk, blk)
        s = _dot(q, kt_ref[:, pl.ds(start, blk)])
        return s - c_ref[0, :, pl.ds(start, blk)], start

    def update(carry, t, start):
        m, l, acc = carry
        m_new = jnp.maximum(m, jnp.max(t, axis=-1, keepdims=True))
        p = jnp.exp(t - m_new)
        a = jnp.exp(m - m_new)
        l = a * l + jnp.sum(p, axis=-1, keepdims=True)
        acc = a * acc + _dot(p.astype(BF16), vb_ref[pl.ds(start, blk), :])
        return m_new, l, acc

    def body(j, carry):
        t, start = scores(j)
        return update(carry, t, start)

    init = (jnp.full((blk, 1), -jnp.inf, F32), jnp.zeros((blk, 1), F32), jnp.zeros((blk, dh), F32))
    carry = lax.fori_loop(0, qi, body, init)
    t, start = scores(qi)
    r = lax.broadcasted_iota(jnp.int32, (blk, blk), 0)
    c = lax.broadcasted_iota(jnp.int32, (blk, blk), 1)
    t = jnp.where(c <= r, t, -jnp.inf)
    _, l, acc = update(carry, t, start)
    o_ref[0] = (acc / l).astype(o_ref.dtype)


def _fox_prompt(q, k, v, c, *, n_heads, blk):
    b, seq, width = q.shape
    dh = width // n_heads
    return pl.pallas_call(
        functools.partial(_fox_prompt_kernel, blk=blk),
        grid=(b, n_heads, seq // blk),
        in_specs=[pl.BlockSpec((1, blk, dh), lambda bi, h, i: (bi, i, h)),
                  pl.BlockSpec((1, seq, dh), lambda bi, h, i: (bi, 0, h)),
                  pl.BlockSpec((1, seq, dh), lambda bi, h, i: (bi, 0, h)),
                  pl.BlockSpec((1, 1, seq), lambda bi, h, i: (bi * n_heads + h, 0, 0))],
        out_specs=pl.BlockSpec((1, blk, dh), lambda bi, h, i: (bi, i, h)),
        out_shape=jax.ShapeDtypeStruct((b, seq, width), BF16),
        scratch_shapes=[pltpu.VMEM((dh, seq), BF16), pltpu.VMEM((seq, dh), BF16)],
        compiler_params=_params("parallel", "parallel", "arbitrary"), name="fox_prompt",
    )(q, k, v, c)


def _fox_sample_kernel(q_ref, kn_ref, vn_ref, kc_ref, vc_ref, c_ref, o_ref):
    past = kc_ref.shape[1]
    t_new = q_ref.shape[1]
    q = q_ref[0]
    s_c = _dot_nt(q, kc_ref[0].astype(BF16)) - c_ref[0, :, 0:past]
    s_n = _dot_nt(q, kn_ref[0].astype(BF16)) - c_ref[0, :, past:past + t_new]
    r = lax.broadcasted_iota(jnp.int32, (t_new, t_new), 0)
    c = lax.broadcasted_iota(jnp.int32, (t_new, t_new), 1)
    s_n = jnp.where(c <= r, s_n, -jnp.inf)
    m = jnp.maximum(jnp.max(s_c, axis=-1, keepdims=True), jnp.max(s_n, axis=-1, keepdims=True))
    p_c = jnp.exp(s_c - m)
    p_n = jnp.exp(s_n - m)
    l = jnp.sum(p_c, axis=-1, keepdims=True) + jnp.sum(p_n, axis=-1, keepdims=True)
    acc = _dot(p_c.astype(BF16), vc_ref[0].astype(BF16)) + _dot(p_n.astype(BF16), vn_ref[0].astype(BF16))
    o_ref[0] = (acc / l).astype(o_ref.dtype)


def _fox_sample(q, kn, vn, kc, vc, c, *, n_heads):
    b, t_new, width = q.shape
    past = kc.shape[1]
    dh = width // n_heads
    new = pl.BlockSpec((1, t_new, dh), lambda bi, h: (bi, 0, h))
    old = pl.BlockSpec((1, past, dh), lambda bi, h: (bi, 0, h))
    return pl.pallas_call(
        _fox_sample_kernel, grid=(b, n_heads),
        in_specs=[new, new, new, old, old,
                  pl.BlockSpec((1, 1, c.shape[2]), lambda bi, h: (bi * n_heads + h, 0, 0))],
        out_specs=new,
        out_shape=jax.ShapeDtypeStruct((b, t_new, width), BF16),
        compiler_params=_params("parallel", "parallel"), name="fox_sample",
    )(q, kn, vn, kc, vc, c)


def _out_proj_kernel(fo_ref, go_ref, x_ref, w_ref, g_ref, b_ref, o_ref, *, alpha):
    half = fo_ref.shape[1]
    mix = _dot(fo_ref[...], w_ref[0:half, :]) + _dot(go_ref[...], w_ref[half:2 * half, :])
    o_ref[...] = _layer_norm(alpha * x_ref[...] + mix, g_ref[...], b_ref[...])


def _out_proj(fo, go, x, w, g, b, *, tm, alpha):
    m, d = x.shape
    row = lambda wd: pl.BlockSpec((tm, wd), lambda i: (i, 0))
    return pl.pallas_call(
        functools.partial(_out_proj_kernel, alpha=alpha), grid=(m // tm,),
        in_specs=[row(fo.shape[1]), row(go.shape[1]), row(d),
                  _resident(w.shape), _resident(g.shape), _resident(b.shape)],
        out_specs=row(d), out_shape=jax.ShapeDtypeStruct((m, d), F32),
        compiler_params=_params("parallel"), name="out_proj_ln",
    )(fo, go, x, w, g, b)


def _proj_kernel(x_ref, w_ref, o_ref, ob_ref):
    y = _dot(x_ref[...].astype(BF16), w_ref[...])
    o_ref[...] = y
    ob_ref[...] = y.astype(BF16)


def _proj(x, w, *, tm, tn):
    m, kd = x.shape
    n = w.shape[1]
    out = pl.BlockSpec((tm, tn), lambda i, j: (i, j))
    return pl.pallas_call(
        _proj_kernel, grid=(m // tm, n // tn),
        in_specs=[pl.BlockSpec((tm, kd), lambda i, j: (i, 0)),
                  pl.BlockSpec((kd, tn), lambda i, j: (0, j))],
        out_specs=[out, out],
        out_shape=[jax.ShapeDtypeStruct((m, n), F32), jax.ShapeDtypeStruct((m, n), BF16)],
        compiler_params=_params("parallel", "parallel"), name="mem_proj",
    )(x, w)


def _mem_attn_kernel(h_ref, wq_ref, wo_ref, mk_ref, mv_ref, g_ref, b_ref, o_ref, att_ref,
                     *, n_heads, alpha, scale):
    nb = mk_ref.shape[0]
    tm, d = h_ref.shape
    rb = tm // nb
    dh = d // n_heads
    h = h_ref[...]
    qb = (_dot(h.astype(BF16), wq_ref[...]) * scale).astype(BF16)
    for bi in range(nb):
        rows = slice(bi * rb, (bi + 1) * rb)
        for hd in range(n_heads):
            cols = slice(hd * dh, (hd + 1) * dh)
            s = _dot_nt(qb[rows, cols], mk_ref[bi, :, cols])
            p = jnp.exp(s - jnp.max(s, axis=-1, keepdims=True))
            l = jnp.sum(p, axis=-1, keepdims=True)
            o = _dot(p.astype(BF16), mv_ref[bi, :, cols])
            att_ref[rows, cols] = (o / l).astype(BF16)
    y = _dot(att_ref[...], wo_ref[...])
    o_ref[...] = _layer_norm(alpha * h + y, g_ref[...], b_ref[...])


def _mem_attn(h, wq, wo, mk, mv, g, b, *, tm, rows_per_batch, n_heads, alpha, scale):
    m, d = h.shape
    n_mem = mk.shape[1]
    if tm <= rows_per_batch:
        nb = 1
        steps_per_batch = rows_per_batch // tm
        mem_idx = lambda i: (i // steps_per_batch, 0, 0)
    else:
        nb = tm // rows_per_batch
        mem_idx = lambda i: (i, 0, 0)
    row = pl.BlockSpec((tm, d), lambda i: (i, 0))
    mem = pl.BlockSpec((nb, n_mem, d), mem_idx)
    return pl.pallas_call(
        functools.partial(_mem_attn_kernel, n_heads=n_heads, alpha=alpha, scale=scale),
        grid=(m // tm,),
        in_specs=[row, _resident(wq.shape), _resident(wo.shape), mem, mem,
                  _resident(g.shape), _resident(b.shape)],
        out_specs=row, out_shape=jax.ShapeDtypeStruct((m, d), F32),
        scratch_shapes=[pltpu.VMEM((tm, d), BF16)],
        compiler_params=_params("parallel"), name="mem_attn_ln",
    )(h, wq, wo, mk, mv, g, b)


def _ffn_kernel(x_ref, wu_ref, wd_ref, g_ref, b_ref, o_ref, xb_ref, *, alpha):
    f = pl.program_id(1)

    @pl.when(f == 0)
    def _():
        xb_ref[...] = x_ref[...].astype(BF16)

    a = jnp.maximum(_dot(xb_ref[...], wu_ref[...]), 0.0)
    part = _dot((a * a).astype(BF16), wd_ref[...])

    @pl.when(f == 0)
    def _():
        o_ref[...] = part

    @pl.when(f > 0)
    def _():
        o_ref[...] += part

    @pl.when(f == pl.num_programs(1) - 1)
    def _():
        o_ref[...] = _layer_norm(alpha * x_ref[...] + o_ref[...], g_ref[...], b_ref[...])


def _ffn(x, wu, wd, g, b, *, tm, tf, alpha):
    m, d = x.shape
    dff = wu.shape[1]
    row = pl.BlockSpec((tm, d), lambda i, f: (i, 0))
    return pl.pallas_call(
        functools.partial(_ffn_kernel, alpha=alpha), grid=(m // tm, dff // tf),
        in_specs=[row, pl.BlockSpec((d, tf), lambda i, f: (0, f)),
                  pl.BlockSpec((tf, d), lambda i, f: (f, 0)),
                  _resident(g.shape), _resident(b.shape)],
        out_specs=row, out_shape=jax.ShapeDtypeStruct((m, d), F32),
        scratch_shapes=[pltpu.VMEM((tm, d), BF16)],
        compiler_params=_params("parallel", "arbitrary"), name="ffn_ln",
    )(x, wu, wd, g, b)


def _pad_lanes(a):
    return jnp.pad(a, ((0, 0), (0, LANES - a.shape[1])))


def kernel(x_prompt, x_sample, mem_prompt, cache_fox_k, cache_fox_v, cache_fox_logf, cache_mem_k, cache_mem_v, w_in, b_f, sgu_ln_g, sgu_ln_b, w_s, b_s, w_out, ln1_g, ln1_b, w_mq, w_mk, w_mv, w_mo, ln2_g, ln2_b, w_up, w_down, ln3_g, ln3_b):
    depth = w_in.shape[0]
    assert depth == 1
    b, seq, d = x_prompt.shape
    bs_, t_new, _ = x_sample.shape
    past, n_heads, dh = cache_fox_k.shape[2:]
    n_mem, mem_heads, mem_dh = cache_mem_k.shape[2:]
    n_groups, chunk = w_s.shape[1], w_s.shape[2]
    fox_w = n_heads * dh
    gw = d - fox_w
    alpha = (2 * depth) ** 0.25
    l = 0

    wi = w_in[l]
    o3 = 3 * fox_w
    o4 = o3 + n_heads
    wqkv = wi[:, :o3].astype(BF16)
    wf = _pad_lanes(wi[:, o3:o4]).astype(BF16)
    bf = _pad_lanes(b_f[l][None, :])
    wug = wi[:, o4:].astype(BF16)
    lg, lb = sgu_ln_g[l][None, :], sgu_ln_b[l][None, :]
    ws_p, bs_p = w_s[l], b_s[l][:, :, None]
    reps = chunk // t_new
    ws_s = jnp.tile(w_s[l][:, :t_new, :t_new], (1, reps, reps))
    bs_s = jnp.tile(b_s[l][:, :t_new], (1, reps))[:, :, None]
    wo = w_out[l].astype(BF16)
    wmq, wmk, wmv, wmo = (w[l].astype(BF16) for w in (w_mq, w_mk, w_mv, w_mo))
    wu, wd = w_up[l].astype(BF16), w_down[l].astype(BF16)
    ln = [a[l][None, :] for a in (ln1_g, ln1_b, ln2_g, ln2_b, ln3_g, ln3_b)]

    def post(h, fo, go, mk, mv, *, rows_per_batch, tm_attn):
        h = _out_proj(fo, go, h, wo, ln[0], ln[1], tm=512, alpha=alpha)
        h = _mem_attn(h, wmq, wmo, mk, mv, ln[2], ln[3], tm=tm_attn, rows_per_batch=rows_per_batch,
                      n_heads=mem_heads, alpha=alpha, scale=mem_dh ** -0.5)
        return _ffn(h, wu, wd, ln[4], ln[5], tm=512, tf=1024, alpha=alpha)

    xp = x_prompt.reshape(b * seq, d)
    q, k, v, logf, go = _in_proj(xp, wqkv, wf, bf, wug, lg, lb, ws_p, bs_p, tm=512, period=chunk,
                                 n_heads=n_heads, q_scale=dh ** -0.5, emit_g=False)
    lt = jnp.transpose(logf.reshape(b, seq, n_heads), (0, 2, 1)).reshape(b * n_heads, seq // LANES, LANES)
    c = _cumsum_rows(lt).reshape(b * n_heads, 1, seq)
    fo = _fox_prompt(q.reshape(b, seq, fox_w), k.reshape(b, seq, fox_w), v.reshape(b, seq, fox_w), c,
                     n_heads=n_heads, blk=256)
    mem2 = mem_prompt.reshape(b * n_mem, d)
    mk, mkb = _proj(mem2, wmk, tm=512, tn=1024)
    mv, mvb = _proj(mem2, wmv, tm=512, tn=1024)
    yp = post(xp, fo.reshape(b * seq, fox_w), go, mkb.reshape(b, n_mem, d), mvb.reshape(b, n_mem, d),
              rows_per_batch=seq, tm_attn=512)

    xs = x_sample.reshape(bs_ * t_new, d)
    qs, ks, vs, logfs, gos, gs = _in_proj(xs, wqkv, wf, bf, wug, lg, lb, ws_s, bs_s, tm=bs_ * t_new,
                                          period=t_new, n_heads=n_heads, q_scale=dh ** -0.5, emit_g=True)
    rows_c = -(-(past + t_new) // LANES)
    rows_c = -(-rows_c // 8) * 8
    lcat = jnp.concatenate([cache_fox_logf[l], logfs.reshape(bs_, t_new, n_heads)], axis=1)
    lcat = jnp.pad(lcat, ((0, 0), (0, rows_c * LANES - past - t_new), (0, 0)))
    lts = jnp.transpose(lcat, (0, 2, 1)).reshape(bs_ * n_heads, rows_c, LANES)
    cs = _cumsum_rows(lts).reshape(bs_ * n_heads, 1, rows_c * LANES)
    fos = _fox_sample(qs.reshape(bs_, t_new, fox_w), ks.reshape(bs_, t_new, fox_w), vs.reshape(bs_, t_new, fox_w),
                      cache_fox_k[l].reshape(bs_, past, fox_w), cache_fox_v[l].reshape(bs_, past, fox_w), cs,
                      n_heads=n_heads)
    cmk = cache_mem_k[l].reshape(bs_, n_mem, d).astype(BF16)
    cmv = cache_mem_v[l].reshape(bs_, n_mem, d).astype(BF16)
    ys = post(xs, fos.reshape(bs_ * t_new, fox_w), gos, cmk, cmv, rows_per_batch=t_new, tm_attn=128)

    return (yp.reshape(b, seq, d), ys.reshape(bs_, t_new, d),
            k.reshape(1, b, seq, n_heads, dh), v.reshape(1, b, seq, n_heads, dh),
            logf.reshape(1, b, seq, n_heads),
            mk.reshape(1, b, n_mem, mem_heads, mem_dh), mv.reshape(1, b, n_mem, mem_heads, mem_dh),
            ks.reshape(1, bs_, t_new, n_heads, dh), vs.reshape(1, bs_, t_new, n_heads, dh),
            logfs.reshape(1, bs_, t_new, n_heads), gs.reshape(1, bs_, t_new, gw))
```

```python
import functools
import math

import jax
import jax.numpy as jnp
from jax import lax
from jax.experimental import pallas as pl
from jax.experimental.pallas import tpu as pltpu

F32 = jnp.float32
BF16 = jnp.bfloat16

LN_EPS = 1e-5
LANES = 128
GELU_C = math.sqrt(2.0 / math.pi)
LOG2E = 1.0 / math.log(2.0)
VMEM_LIMIT = 56 * 1024 * 1024


def _dot(a, b):
    return jnp.dot(a, b, preferred_element_type=F32)


def _dot_nt(a, b):
    return lax.dot_general(a, b, (((1,), (1,)), ((), ())), preferred_element_type=F32)


def _gelu(x):
    return 0.5 * x * (1.0 + jnp.tanh(GELU_C * (x + 0.044715 * (x * x * x))))


def _layer_norm(x, g, b):
    mu = jnp.mean(x, axis=-1, keepdims=True)
    xc = x - mu
    var = jnp.mean(xc * xc, axis=-1, keepdims=True)
    return xc * lax.rsqrt(var + LN_EPS) * g + b


def _log_sigmoid(x):
    return jnp.minimum(x, 0.0) - jnp.log1p(jnp.exp(-jnp.abs(x)))


def _resident(shape):
    nd = len(shape)
    return pl.BlockSpec(shape, lambda *_: (0,) * nd, pipeline_mode=pl.Buffered(1))


def _params(*sem):
    return pltpu.CompilerParams(dimension_semantics=sem, vmem_limit_bytes=VMEM_LIMIT)


def _in_proj_kernel(x_ref, wqkv_ref, wf_ref, bf_ref, wug_ref, lg_ref, lb_ref, ws_ref, bs_ref,
                    q_ref, k_ref, v_ref, logf_ref, go_ref, *g_ref,
                    fox_w, n_heads, gw, n_groups, period, q_scale):
    tm = x_ref.shape[0]
    chunk = ws_ref.shape[1]
    gd = gw // n_groups
    xb = x_ref[...].astype(BF16)
    q_ref[...] = (_dot(xb, wqkv_ref[:, 0:fox_w]) * q_scale).astype(BF16)
    k_ref[...] = _dot(xb, wqkv_ref[:, fox_w:2 * fox_w])
    v_ref[...] = _dot(xb, wqkv_ref[:, 2 * fox_w:3 * fox_w])
    zf = _dot(xb, wf_ref[...]) + bf_ref[...]
    logf_ref[...] = _log_sigmoid(zf)[:, :n_heads]
    g = _layer_norm(_gelu(_dot(xb, wug_ref[:, gw:2 * gw])), lg_ref[...], lb_ref[...])
    if g_ref:
        g_ref[0][...] = g
    gb = g.astype(BF16)
    u = _gelu(_dot(xb, wug_ref[:, 0:gw]))
    r = lax.broadcasted_iota(jnp.int32, (chunk, chunk), 0)
    c = lax.broadcasted_iota(jnp.int32, (chunk, chunk), 1)
    sh = period.bit_length() - 1
    keep = ((r >> sh) == (c >> sh)) & ((c & (period - 1)) <= (r & (period - 1)))
    for gi in range(n_groups):
        wsg = jnp.where(keep, ws_ref[gi], 0.0).astype(BF16)
        cols = slice(gi * gd, (gi + 1) * gd)
        for ci in range(tm // chunk):
            rows = slice(ci * chunk, (ci + 1) * chunk)
            s = _dot(wsg, gb[rows, cols]) + bs_ref[gi]
            go_ref[rows, cols] = (u[rows, cols] * s).astype(BF16)


def _in_proj(x, wqkv, wf, bf, wug, lg, lb, ws, bs, *, tm, period, n_heads, q_scale, emit_g):
    m, d = x.shape
    fox_w = wqkv.shape[1] // 3
    gw = wug.shape[1] // 2
    n_groups = ws.shape[0]
    row = lambda w: pl.BlockSpec((tm, w), lambda i: (i, 0))
    out_shape = [jax.ShapeDtypeStruct((m, fox_w), BF16),
                 jax.ShapeDtypeStruct((m, fox_w), F32),
                 jax.ShapeDtypeStruct((m, fox_w), F32),
                 jax.ShapeDtypeStruct((m, n_heads), F32),
                 jax.ShapeDtypeStruct((m, gw), BF16)]
    out_specs = [row(fox_w), row(fox_w), row(fox_w), row(n_heads), row(gw)]
    if emit_g:
        out_shape.append(jax.ShapeDtypeStruct((m, gw), F32))
        out_specs.append(row(gw))
    kern = functools.partial(_in_proj_kernel, fox_w=fox_w, n_heads=n_heads, gw=gw,
                             n_groups=n_groups, period=period, q_scale=q_scale)
    return pl.pallas_call(
        kern, grid=(m // tm,),
        in_specs=[row(d), _resident(wqkv.shape), _resident(wf.shape), _resident(bf.shape),
                  _resident(wug.shape), _resident(lg.shape), _resident(lb.shape),
                  _resident(ws.shape), _resident(bs.shape)],
        out_specs=out_specs, out_shape=out_shape,
        compiler_params=_params("parallel"), name="in_proj_sgu",
    )(x, wqkv, wf, bf, wug, lg, lb, ws, bs)


def _cumsum_kernel(x_ref, o_ref, *, rows):
    x = x_ref[...]
    n = x.shape[0]
    hi = lax.Precision.HIGHEST
    ii = lax.broadcasted_iota(jnp.int32, (LANES, LANES), 0)
    jj = lax.broadcasted_iota(jnp.int32, (LANES, LANES), 1)
    within = jnp.dot(x, (ii <= jj).astype(F32), precision=hi, preferred_element_type=F32)
    totals = jnp.dot(x, jnp.ones((LANES, LANES), F32), precision=hi, preferred_element_type=F32)
    ri = lax.broadcasted_iota(jnp.int32, (n, n), 0)
    rj = lax.broadcasted_iota(jnp.int32, (n, n), 1)
    seq_i = jnp.zeros((n, n), jnp.int32)
    seq_j = jnp.zeros((n, n), jnp.int32)
    for s in range(rows, n, rows):
        seq_i += (ri >= s).astype(jnp.int32)
        seq_j += (rj >= s).astype(jnp.int32)
    earlier = ((seq_i == seq_j) & (rj < ri)).astype(F32)
    o_ref[...] = within + jnp.dot(earlier, totals, precision=hi, preferred_element_type=F32)


def _cumsum_rows(x, *, per_step):
    n, rows, _ = x.shape
    x2 = x.reshape(n * rows, LANES)
    spec = pl.BlockSpec((per_step * rows, LANES), lambda i: (i, 0))
    out = pl.pallas_call(
        functools.partial(_cumsum_kernel, rows=rows), grid=(n // per_step,),
        in_specs=[spec], out_specs=spec,
        out_shape=jax.ShapeDtypeStruct(x2.shape, F32),
        compiler_params=_params("parallel"), name="logf_cumsum",
    )(x2)
    return out.reshape(n, rows * LANES)


def _fox_prompt_kernel(q_ref, k_ref, v_ref, c_ref, o_ref, ka_ref, vt_ref, qa_ref, sa_ref, sb_ref,
                       m_ref, l_ref, acc_ref, *, hk, hpb):
    qi = pl.program_id(2)
    tq = q_ref.shape[1]
    seq = k_ref.shape[1]
    dh = vt_ref.shape[1]
    prep = 512

    @pl.when(qi == 0)
    def _():
        row = lax.broadcasted_iota(jnp.int32, (dh, prep), 0)
        for hh in range(hpb):
            hc = slice(hh * dh, (hh + 1) * dh)
            for r0 in range(0, seq, prep):
                rs = slice(r0, r0 + prep)
                ka_ref[hh, rs, 0:dh] = k_ref[0, rs, hc].astype(BF16)
                vt_ref[hh, :, rs] = v_ref[0, rs, hc].T.astype(BF16)
                c2 = c_ref[0, hh:hh + 1, rs] * LOG2E
                hi = c2.astype(BF16).astype(F32)
                r1 = c2 - hi
                mid = r1.astype(BF16).astype(F32)
                lo = r1 - mid
                aug = jnp.where(row == 0, -hi, jnp.where(row == 1, -mid, jnp.where(row == 2, -lo, 0.0)))
                ka_ref[hh, rs, dh:2 * dh] = aug.T.astype(BF16)

    ones = (lax.broadcasted_iota(jnp.int32, (dh, tq), 0) < 3).astype(BF16)
    for hh in range(hpb):
        qa_ref[hh, 0:dh, :] = q_ref[0, :, hh * dh:(hh + 1) * dh].astype(F32).T.astype(BF16)
        qa_ref[hh, dh:2 * dh, :] = ones

    def qk(hh, blk):
        start = pl.multiple_of(blk * hk, hk)
        return _dot(ka_ref[hh, pl.ds(start, hk), :], qa_ref[hh])

    def process(hh, s, blk):
        start = pl.multiple_of(blk * hk, hk)
        m = m_ref[hh]
        m_new = jnp.maximum(m, jnp.max(s, axis=0, keepdims=True))
        p = jnp.exp2(s - m_new)
        a = jnp.exp2(m - m_new)
        m_ref[hh] = m_new
        l_ref[hh] = a * l_ref[hh] + jnp.sum(p, axis=0, keepdims=True)
        acc_ref[hh] = a * acc_ref[hh] + _dot(vt_ref[hh, :, pl.ds(start, hk)], p.astype(BF16))

    for hh in range(hpb):
        m_ref[hh] = jnp.full((1, tq), -jnp.inf, F32)
        l_ref[hh] = jnp.zeros((1, tq), F32)
        acc_ref[hh] = jnp.zeros((dh, tq), F32)
        sa_ref[hh] = qk(hh, 0)

    def body(j, carry):
        for hh in range(hpb):
            sb_ref[hh] = qk(hh, 2 * j + 1)
            process(hh, sa_ref[hh], 2 * j)
            sa_ref[hh] = qk(hh, 2 * j + 2)
            process(hh, sb_ref[hh], 2 * j + 1)
        return carry

    lax.fori_loop(0, qi, body, 0)
    kidx = lax.broadcasted_iota(jnp.int32, (hk, tq), 0)
    qidx = lax.broadcasted_iota(jnp.int32, (hk, tq), 1)
    for hh in range(hpb):
        sb_ref[hh] = qk(hh, 2 * qi + 1)
        process(hh, jnp.where(kidx <= qidx, sa_ref[hh], -jnp.inf), 2 * qi)
        process(hh, jnp.where(kidx + hk <= qidx, sb_ref[hh], -jnp.inf), 2 * qi + 1)
        o_ref[0, :, hh * dh:(hh + 1) * dh] = (acc_ref[hh] / l_ref[hh]).T.astype(o_ref.dtype)


def _fox_prompt(q, k, v, c, *, n_heads, hpb):
    b, seq, width = q.shape
    dh = width // n_heads
    hk = 2 * LANES
    tq = 2 * hk
    groups = n_heads // hpb
    cg = c.reshape(b * groups, hpb, seq)
    wide = hpb * dh
    return pl.pallas_call(
        functools.partial(_fox_prompt_kernel, hk=hk, hpb=hpb),
        grid=(b, groups, seq // tq),
        in_specs=[pl.BlockSpec((1, tq, wide), lambda bi, h, i: (bi, i, h)),
                  pl.BlockSpec((1, seq, wide), lambda bi, h, i: (bi, 0, h)),
                  pl.BlockSpec((1, seq, wide), lambda bi, h, i: (bi, 0, h)),
                  pl.BlockSpec((1, hpb, seq), lambda bi, h, i: (bi * groups + h, 0, 0))],
        out_specs=pl.BlockSpec((1, tq, wide), lambda bi, h, i: (bi, i, h)),
        out_shape=jax.ShapeDtypeStruct((b, seq, width), BF16),
        scratch_shapes=[pltpu.VMEM((hpb, seq, 2 * dh), BF16), pltpu.VMEM((hpb, dh, seq), BF16),
                        pltpu.VMEM((hpb, 2 * dh, tq), BF16),
                        pltpu.VMEM((hpb, hk, tq), F32), pltpu.VMEM((hpb, hk, tq), F32),
                        pltpu.VMEM((hpb, 1, tq), F32), pltpu.VMEM((hpb, 1, tq), F32),
                        pltpu.VMEM((hpb, dh, tq), F32)],
        compiler_params=_params("parallel", "parallel", "arbitrary"), name="fox_prompt",
    )(q, k, v, cg)


def _fox_sample_kernel(q_ref, kn_ref, vn_ref, kc_ref, vc_ref, c_ref, o_ref):
    past = kc_ref.shape[1]
    t_new = q_ref.shape[1]
    q = q_ref[0]
    s_c = _dot_nt(q, kc_ref[0].astype(BF16)) - c_ref[0, :, 0:past]
    s_n = _dot_nt(q, kn_ref[0].astype(BF16)) - c_ref[0, :, past:past + t_new]
    r = lax.broadcasted_iota(jnp.int32, (t_new, t_new), 0)
    c = lax.broadcasted_iota(jnp.int32, (t_new, t_new), 1)
    s_n = jnp.where(c <= r, s_n, -jnp.inf)
    m = jnp.maximum(jnp.max(s_c, axis=-1, keepdims=True), jnp.max(s_n, axis=-1, keepdims=True))
    p_c = jnp.exp(s_c - m)
    p_n = jnp.exp(s_n - m)
    l = jnp.sum(p_c, axis=-1, keepdims=True) + jnp.sum(p_n, axis=-1, keepdims=True)
    acc = _dot(p_c.astype(BF16), vc_ref[0].astype(BF16)) + _dot(p_n.astype(BF16), vn_ref[0].astype(BF16))
    o_ref[0] = (acc / l).astype(o_ref.dtype)


def _fox_sample(q, kn, vn, kc, vc, c, *, n_heads):
    b, t_new, width = q.shape
    past = kc.shape[1]
    dh = width // n_heads
    new = pl.BlockSpec((1, t_new, dh), lambda bi, h: (bi, 0, h))
    old = pl.BlockSpec((1, past, dh), lambda bi, h: (bi, 0, h))
    return pl.pallas_call(
        _fox_sample_kernel, grid=(b, n_heads),
        in_specs=[new, new, new, old, old,
                  pl.BlockSpec((1, 1, c.shape[2]), lambda bi, h: (bi * n_heads + h, 0, 0))],
        out_specs=new,
        out_shape=jax.ShapeDtypeStruct((b, t_new, width), BF16),
        compiler_params=_params("parallel", "parallel"), name="fox_sample",
    )(q, kn, vn, kc, vc, c)


def _out_proj_kernel(fo_ref, go_ref, x_ref, w_ref, g_ref, b_ref, o_ref, *, alpha):
    half = fo_ref.shape[1]
    mix = _dot(fo_ref[...], w_ref[0:half, :]) + _dot(go_ref[...], w_ref[half:2 * half, :])
    o_ref[...] = _layer_norm(alpha * x_ref[...] + mix, g_ref[...], b_ref[...])


def _out_proj(fo, go, x, w, g, b, *, tm, alpha):
    m, d = x.shape
    row = lambda wd: pl.BlockSpec((tm, wd), lambda i: (i, 0))
    return pl.pallas_call(
        functools.partial(_out_proj_kernel, alpha=alpha), grid=(m // tm,),
        in_specs=[row(fo.shape[1]), row(go.shape[1]), row(d),
                  _resident(w.shape), _resident(g.shape), _resident(b.shape)],
        out_specs=row(d), out_shape=jax.ShapeDtypeStruct((m, d), F32),
        compiler_params=_params("parallel"), name="out_proj_ln",
    )(fo, go, x, w, g, b)


def _proj_kernel(x_ref, w_ref, o_ref, ob_ref):
    y = _dot(x_ref[...].astype(BF16), w_ref[...])
    o_ref[...] = y
    ob_ref[...] = y.astype(BF16)


def _proj(x, w, *, tm, tn):
    m, kd = x.shape
    n = w.shape[1]
    out = pl.BlockSpec((tm, tn), lambda i, j: (i, j))
    return pl.pallas_call(
        _proj_kernel, grid=(m // tm, n // tn),
        in_specs=[pl.BlockSpec((tm, kd), lambda i, j: (i, 0)),
                  pl.BlockSpec((kd, tn), lambda i, j: (0, j))],
        out_specs=[out, out],
        out_shape=[jax.ShapeDtypeStruct((m, n), F32), jax.ShapeDtypeStruct((m, n), BF16)],
        compiler_params=_params("parallel", "parallel"), name="mem_proj",
    )(x, w)


def _mem_attn_kernel(h_ref, wq_ref, wo_ref, mk_ref, mv_ref, g_ref, b_ref, o_ref, att_ref,
                     *, n_heads, alpha, scale):
    nb = mk_ref.shape[0]
    tm, d = h_ref.shape
    rb = tm // nb
    dh = d // n_heads
    h = h_ref[...]
    qb = (_dot(h.astype(BF16), wq_ref[...]) * scale).astype(BF16)
    for bi in range(nb):
        rows = slice(bi * rb, (bi + 1) * rb)
        for hd in range(n_heads):
            cols = slice(hd * dh, (hd + 1) * dh)
            s = _dot_nt(qb[rows, cols], mk_ref[bi, :, cols])
            p = jnp.exp(s - jnp.max(s, axis=-1, keepdims=True))
            l = jnp.sum(p, axis=-1, keepdims=True)
            o = _dot(p.astype(BF16), mv_ref[bi, :, cols])
            att_ref[rows, cols] = (o / l).astype(BF16)
    y = _dot(att_ref[...], wo_ref[...])
    o_ref[...] = _layer_norm(alpha * h + y, g_ref[...], b_ref[...])


def _mem_attn(h, wq, wo, mk, mv, g, b, *, tm, rows_per_batch, n_heads, alpha, scale):
    m, d = h.shape
    n_mem = mk.shape[1]
    if tm <= rows_per_batch:
        nb = 1
        steps_per_batch = rows_per_batch // tm
        mem_idx = lambda i: (i // steps_per_batch, 0, 0)
    else:
        nb = tm // rows_per_batch
        mem_idx = lambda i: (i, 0, 0)
    row = pl.BlockSpec((tm, d), lambda i: (i, 0))
    mem = pl.BlockSpec((nb, n_mem, d), mem_idx)
    return pl.pallas_call(
        functools.partial(_mem_attn_kernel, n_heads=n_heads, alpha=alpha, scale=scale),
        grid=(m // tm,),
        in_specs=[row, _resident(wq.shape), _resident(wo.shape), mem, mem,
                  _resident(g.shape), _resident(b.shape)],
        out_specs=row, out_shape=jax.ShapeDtypeStruct((m, d), F32),
        scratch_shapes=[pltpu.VMEM((tm, d), BF16)],
        compiler_params=_params("parallel"), name="mem_attn_ln",
    )(h, wq, wo, mk, mv, g, b)


def _ffn_kernel(x_ref, wu_ref, wd_ref, g_ref, b_ref, o_ref, xb_ref, *, alpha):
    f = pl.program_id(1)

    @pl.when(f == 0)
    def _():
        xb_ref[...] = x_ref[...].astype(BF16)

    a = jnp.maximum(_dot(xb_ref[...], wu_ref[...]), 0.0)
    part = _dot((a * a).astype(BF16), wd_ref[...])

    @pl.when(f == 0)
    def _():
        o_ref[...] = part

    @pl.when(f > 0)
    def _():
        o_ref[...] += part

    @pl.when(f == pl.num_programs(1) - 1)
    def _():
        o_ref[...] = _layer_norm(alpha * x_ref[...] + o_ref[...], g_ref[...], b_ref[...])


def _ffn(x, wu, wd, g, b, *, tm, tf, alpha):
    m, d = x.shape
    dff = wu.shape[1]
    row = pl.BlockSpec((tm, d), lambda i, f: (i, 0))
    return pl.pallas_call(
        functools.partial(_ffn_kernel, alpha=alpha), grid=(m // tm, dff // tf),
        in_specs=[row, pl.BlockSpec((d, tf), lambda i, f: (0, f)),
                  pl.BlockSpec((tf, d), lambda i, f: (f, 0)),
                  _resident(g.shape), _resident(b.shape)],
        out_specs=row, out_shape=jax.ShapeDtypeStruct((m, d), F32),
        scratch_shapes=[pltpu.VMEM((tm, d), BF16)],
        compiler_params=_params("parallel", "arbitrary"), name="ffn_ln",
    )(x, wu, wd, g, b)


def _pad_lanes(a):
    return jnp.pad(a, ((0, 0), (0, LANES - a.shape[1])))


def kernel(x_prompt, x_sample, mem_prompt, cache_fox_k, cache_fox_v, cache_fox_logf, cache_mem_k, cache_mem_v, w_in, b_f, sgu_ln_g, sgu_ln_b, w_s, b_s, w_out, ln1_g, ln1_b, w_mq, w_mk, w_mv, w_mo, ln2_g, ln2_b, w_up, w_down, ln3_g, ln3_b):
    depth = w_in.shape[0]
    assert depth == 1
    b, seq, d = x_prompt.shape
    bs_, t_new, _ = x_sample.shape
    past, n_heads, dh = cache_fox_k.shape[2:]
    n_mem, mem_heads, mem_dh = cache_mem_k.shape[2:]
    n_groups, chunk = w_s.shape[1], w_s.shape[2]
    fox_w = n_heads * dh
    gw = d - fox_w
    alpha = (2 * depth) ** 0.25
    l = 0

    wi = w_in[l]
    o3 = 3 * fox_w
    o4 = o3 + n_heads
    wqkv = wi[:, :o3].astype(BF16)
    wf = _pad_lanes(wi[:, o3:o4]).astype(BF16)
    bf = _pad_lanes(b_f[l][None, :])
    wug = wi[:, o4:].astype(BF16)
    lg, lb = sgu_ln_g[l][None, :], sgu_ln_b[l][None, :]
    ws_p, bs_p = w_s[l], b_s[l][:, :, None]
    reps = chunk // t_new
    ws_s = jnp.tile(w_s[l][:, :t_new, :t_new], (1, reps, reps))
    bs_s = jnp.tile(b_s[l][:, :t_new], (1, reps))[:, :, None]
    wo = w_out[l].astype(BF16)
    wmq, wmk, wmv, wmo = (w[l].astype(BF16) for w in (w_mq, w_mk, w_mv, w_mo))
    wu, wd = w_up[l].astype(BF16), w_down[l].astype(BF16)
    ln = [a[l][None, :] for a in (ln1_g, ln1_b, ln2_g, ln2_b, ln3_g, ln3_b)]

    def post(h, fo, go, mk, mv, *, rows_per_batch, tm_attn):
        h = _out_proj(fo, go, h, wo, ln[0], ln[1], tm=512, alpha=alpha)
        h = _mem_attn(h, wmq, wmo, mk, mv, ln[2], ln[3], tm=tm_attn, rows_per_batch=rows_per_batch,
                      n_heads=mem_heads, alpha=alpha, scale=mem_dh ** -0.5)
        return _ffn(h, wu, wd, ln[4], ln[5], tm=512, tf=1024, alpha=alpha)

    xp = x_prompt.reshape(b * seq, d)
    q, k, v, logf, go = _in_proj(xp, wqkv, wf, bf, wug, lg, lb, ws_p, bs_p, tm=512, period=chunk,
                                 n_heads=n_heads, q_scale=dh ** -0.5 * LOG2E, emit_g=False)
    lt = jnp.transpose(logf.reshape(b, seq, n_heads), (0, 2, 1)).reshape(b * n_heads, seq // LANES, LANES)
    c = _cumsum_rows(lt, per_step=n_heads)
    fo = _fox_prompt(q.reshape(b, seq, fox_w), k.reshape(b, seq, fox_w), v.reshape(b, seq, fox_w), c,
                     n_heads=n_heads, hpb=2)
    mem2 = mem_prompt.reshape(b * n_mem, d)
    mk, mkb = _proj(mem2, wmk, tm=512, tn=1024)
    mv, mvb = _proj(mem2, wmv, tm=512, tn=1024)
    yp = post(xp, fo.reshape(b * seq, fox_w), go, mkb.reshape(b, n_mem, d), mvb.reshape(b, n_mem, d),
              rows_per_batch=seq, tm_attn=512)

    xs = x_sample.reshape(bs_ * t_new, d)
    qs, ks, vs, logfs, gos, gs = _in_proj(xs, wqkv, wf, bf, wug, lg, lb, ws_s, bs_s, tm=bs_ * t_new,
                                          period=t_new, n_heads=n_heads, q_scale=dh ** -0.5, emit_g=True)
    rows_c = -(-(past + t_new) // LANES)
    rows_c = -(-rows_c // 8) * 8
    lcat = jnp.concatenate([cache_fox_logf[l], logfs.reshape(bs_, t_new, n_heads)], axis=1)
    lcat = jnp.pad(lcat, ((0, 0), (0, rows_c * LANES - past - t_new), (0, 0)))
    lts = jnp.transpose(lcat, (0, 2, 1)).reshape(bs_ * n_heads, rows_c, LANES)
    cs = _cumsum_rows(lts, per_step=n_heads).reshape(bs_ * n_heads, 1, rows_c * LANES)
    fos = _fox_sample(qs.reshape(bs_, t_new, fox_w), ks.reshape(bs_, t_new, fox_w), vs.reshape(bs_, t_new, fox_w),
                      cache_fox_k.reshape(bs_, past, fox_w), cache_fox_v.reshape(bs_, past, fox_w), cs,
                      n_heads=n_heads)
    cmk = cache_mem_k[l].reshape(bs_, n_mem, d).astype(BF16)
    cmv = cache_mem_v[l].reshape(bs_, n_mem, d).astype(BF16)
    ys = post(xs, fos.reshape(bs_ * t_new, fox_w), gos, cmk, cmv, rows_per_batch=t_new, tm_attn=128)

    return (yp.reshape(b, seq, d), ys.reshape(bs_, t_new, d),
            k.reshape(1, b, seq, n_heads, dh), v.reshape(1, b, seq, n_heads, dh),
            logf.reshape(1, b, seq, n_heads),
            mk.reshape(1, b, n_mem, mem_heads, mem_dh), mv.reshape(1, b, n_mem, mem_heads, mem_dh),
            ks.reshape(1, bs_, t_new, n_heads, dh), vs.reshape(1, bs_, t_new, n_heads, dh),
            logfs.reshape(1, bs_, t_new, n_heads), gs.reshape(1, bs_, t_new, gw))
```

```python
import functools
import math

import jax
import jax.numpy as jnp
from jax import lax
from jax.experimental import pallas as pl
from jax.experimental.pallas import tpu as pltpu

F32 = jnp.float32
BF16 = jnp.bfloat16

LN_EPS = 1e-5
LANES = 128
GELU_C = math.sqrt(2.0 / math.pi)
LOG2E = 1.0 / math.log(2.0)
VMEM_LIMIT = 56 * 1024 * 1024


def _dot(a, b):
    return jnp.dot(a, b, preferred_element_type=F32)


def _dot_nt(a, b):
    return lax.dot_general(a, b, (((1,), (1,)), ((), ())), preferred_element_type=F32)


def _gelu(x):
    return 0.5 * x * (1.0 + jnp.tanh(GELU_C * (x + 0.044715 * (x * x * x))))


def _layer_norm(x, g, b):
    mu = jnp.mean(x, axis=-1, keepdims=True)
    xc = x - mu
    var = jnp.mean(xc * xc, axis=-1, keepdims=True)
    return xc * lax.rsqrt(var + LN_EPS) * g + b


def _log_sigmoid(x):
    return jnp.minimum(x, 0.0) - jnp.log1p(jnp.exp(-jnp.abs(x)))


def _resident(shape):
    nd = len(shape)
    return pl.BlockSpec(shape, lambda *_: (0,) * nd, pipeline_mode=pl.Buffered(1))


def _params(*sem):
    return pltpu.CompilerParams(dimension_semantics=sem, vmem_limit_bytes=VMEM_LIMIT)


def _in_proj_kernel(x_ref, wqkv_ref, wf_ref, bf_ref, wug_ref, lg_ref, lb_ref, ws_ref, bs_ref,
                    q_ref, k_ref, v_ref, logf_ref, go_ref, *g_ref,
                    fox_w, n_heads, gw, n_groups, period, q_scale):
    tm = x_ref.shape[0]
    chunk = ws_ref.shape[1]
    gd = gw // n_groups
    xb = x_ref[...].astype(BF16)
    q_ref[...] = (_dot(xb, wqkv_ref[:, 0:fox_w]) * q_scale).astype(BF16)
    k_ref[...] = _dot(xb, wqkv_ref[:, fox_w:2 * fox_w])
    v_ref[...] = _dot(xb, wqkv_ref[:, 2 * fox_w:3 * fox_w])
    zf = _dot(xb, wf_ref[...]) + bf_ref[...]
    logf_ref[...] = _log_sigmoid(zf)[:, :n_heads]
    g = _layer_norm(_gelu(_dot(xb, wug_ref[:, gw:2 * gw])), lg_ref[...], lb_ref[...])
    if g_ref:
        g_ref[0][...] = g
    gb = g.astype(BF16)
    u = _gelu(_dot(xb, wug_ref[:, 0:gw]))
    r = lax.broadcasted_iota(jnp.int32, (chunk, chunk), 0)
    c = lax.broadcasted_iota(jnp.int32, (chunk, chunk), 1)
    sh = period.bit_length() - 1
    keep = ((r >> sh) == (c >> sh)) & ((c & (period - 1)) <= (r & (period - 1)))
    for gi in range(n_groups):
        wsg = jnp.where(keep, ws_ref[gi], 0.0).astype(BF16)
        cols = slice(gi * gd, (gi + 1) * gd)
        for ci in range(tm // chunk):
            rows = slice(ci * chunk, (ci + 1) * chunk)
            s = _dot(wsg, gb[rows, cols]) + bs_ref[gi]
            go_ref[rows, cols] = (u[rows, cols] * s).astype(BF16)


def _in_proj(x, wqkv, wf, bf, wug, lg, lb, ws, bs, *, tm, period, n_heads, q_scale, emit_g):
    m, d = x.shape
    fox_w = wqkv.shape[1] // 3
    gw = wug.shape[1] // 2
    n_groups = ws.shape[0]
    row = lambda w: pl.BlockSpec((tm, w), lambda i: (i, 0))
    out_shape = [jax.ShapeDtypeStruct((m, fox_w), BF16),
                 jax.ShapeDtypeStruct((m, fox_w), F32),
                 jax.ShapeDtypeStruct((m, fox_w), F32),
                 jax.ShapeDtypeStruct((m, n_heads), F32),
                 jax.ShapeDtypeStruct((m, gw), BF16)]
    out_specs = [row(fox_w), row(fox_w), row(fox_w), row(n_heads), row(gw)]
    if emit_g:
        out_shape.append(jax.ShapeDtypeStruct((m, gw), F32))
        out_specs.append(row(gw))
    kern = functools.partial(_in_proj_kernel, fox_w=fox_w, n_heads=n_heads, gw=gw,
                             n_groups=n_groups, period=period, q_scale=q_scale)
    return pl.pallas_call(
        kern, grid=(m // tm,),
        in_specs=[row(d), _resident(wqkv.shape), _resident(wf.shape), _resident(bf.shape),
                  _resident(wug.shape), _resident(lg.shape), _resident(lb.shape),
                  _resident(ws.shape), _resident(bs.shape)],
        out_specs=out_specs, out_shape=out_shape,
        compiler_params=_params("parallel"), name="in_proj_sgu",
    )(x, wqkv, wf, bf, wug, lg, lb, ws, bs)


def _cumsum_kernel(x_ref, o_ref, *, rows):
    x = x_ref[...]
    n = x.shape[0]
    hi = lax.Precision.HIGHEST
    ii = lax.broadcasted_iota(jnp.int32, (LANES, LANES), 0)
    jj = lax.broadcasted_iota(jnp.int32, (LANES, LANES), 1)
    within = jnp.dot(x, (ii <= jj).astype(F32), precision=hi, preferred_element_type=F32)
    totals = jnp.dot(x, jnp.ones((LANES, LANES), F32), precision=hi, preferred_element_type=F32)
    ri = lax.broadcasted_iota(jnp.int32, (n, n), 0)
    rj = lax.broadcasted_iota(jnp.int32, (n, n), 1)
    seq_i = jnp.zeros((n, n), jnp.int32)
    seq_j = jnp.zeros((n, n), jnp.int32)
    for s in range(rows, n, rows):
        seq_i += (ri >= s).astype(jnp.int32)
        seq_j += (rj >= s).astype(jnp.int32)
    earlier = ((seq_i == seq_j) & (rj < ri)).astype(F32)
    o_ref[...] = within + jnp.dot(earlier, totals, precision=hi, preferred_element_type=F32)


def _cumsum_rows(x, *, per_step):
    n, rows, _ = x.shape
    x2 = x.reshape(n * rows, LANES)
    spec = pl.BlockSpec((per_step * rows, LANES), lambda i: (i, 0))
    out = pl.pallas_call(
        functools.partial(_cumsum_kernel, rows=rows), grid=(n // per_step,),
        in_specs=[spec], out_specs=spec,
        out_shape=jax.ShapeDtypeStruct(x2.shape, F32),
        compiler_params=_params("parallel"), name="logf_cumsum",
    )(x2)
    return out.reshape(n, rows * LANES)


def _fox_prompt_kernel(q_ref, k_ref, v_ref, c_ref, o_ref, ka_ref, vt_ref, qa_ref, sa_ref, sb_ref,
                       pa_ref, pb_ref, aa_ref, ab_ref, m_ref, l_ref, acc_ref, *, hk, hpb):
    qi = pl.program_id(2)
    tq = q_ref.shape[1]
    seq = k_ref.shape[1]
    dh = vt_ref.shape[1]
    prep = 512

    @pl.when(qi == 0)
    def _():
        row = lax.broadcasted_iota(jnp.int32, (dh, prep), 0)
        for hh in range(hpb):
            hc = slice(hh * dh, (hh + 1) * dh)
            for r0 in range(0, seq, prep):
                rs = slice(r0, r0 + prep)
                ka_ref[hh, rs, 0:dh] = k_ref[0, rs, hc].astype(BF16)
                vt_ref[hh, :, rs] = v_ref[0, rs, hc].T.astype(BF16)
                c2 = c_ref[0, hh:hh + 1, rs] * LOG2E
                hi = c2.astype(BF16).astype(F32)
                r1 = c2 - hi
                mid = r1.astype(BF16).astype(F32)
                lo = r1 - mid
                aug = jnp.where(row == 0, -hi, jnp.where(row == 1, -mid, jnp.where(row == 2, -lo, 0.0)))
                ka_ref[hh, rs, dh:2 * dh] = aug.T.astype(BF16)

    ones = (lax.broadcasted_iota(jnp.int32, (dh, tq), 0) < 3).astype(BF16)
    for hh in range(hpb):
        qa_ref[hh, 0:dh, :] = q_ref[0, :, hh * dh:(hh + 1) * dh].astype(F32).T.astype(BF16)
        qa_ref[hh, dh:2 * dh, :] = ones

    def qk(hh, blk):
        start = pl.multiple_of(blk * hk, hk)
        return _dot(ka_ref[hh, pl.ds(start, hk), :], qa_ref[hh])

    def softmax(hh, s, p_ref, a_ref):
        m = m_ref[hh]
        m_new = jnp.maximum(m, jnp.max(s, axis=0, keepdims=True))
        p = jnp.exp2(s - m_new)
        a = jnp.exp2(m - m_new)
        m_ref[hh] = m_new
        l_ref[hh] = a * l_ref[hh] + jnp.sum(p, axis=0, keepdims=True)
        p_ref[hh] = p.astype(BF16)
        a_ref[hh] = a

    def pv(hh, blk, p_ref, a_ref):
        start = pl.multiple_of(blk * hk, hk)
        acc_ref[hh] = a_ref[hh] * acc_ref[hh] + _dot(vt_ref[hh, :, pl.ds(start, hk)], p_ref[hh])

    for hh in range(hpb):
        m_ref[hh] = jnp.full((1, tq), -jnp.inf, F32)
        l_ref[hh] = jnp.zeros((1, tq), F32)
        acc_ref[hh] = jnp.zeros((dh, tq), F32)
        pb_ref[hh] = jnp.zeros((hk, tq), BF16)
        ab_ref[hh] = jnp.ones((1, tq), F32)
        sa_ref[hh] = qk(hh, 0)

    def body(j, carry):
        for hh in range(hpb):
            sb_ref[hh] = qk(hh, 2 * j + 1)
            pv(hh, jnp.maximum(2 * j - 1, 0), pb_ref, ab_ref)
            softmax(hh, sa_ref[hh], pa_ref, aa_ref)
            sa_ref[hh] = qk(hh, 2 * j + 2)
            pv(hh, 2 * j, pa_ref, aa_ref)
            softmax(hh, sb_ref[hh], pb_ref, ab_ref)
        return carry

    lax.fori_loop(0, qi, body, 0)
    kidx = lax.broadcasted_iota(jnp.int32, (hk, tq), 0)
    qidx = lax.broadcasted_iota(jnp.int32, (hk, tq), 1)
    for hh in range(hpb):
        sb_ref[hh] = qk(hh, 2 * qi + 1)
        pv(hh, jnp.maximum(2 * qi - 1, 0), pb_ref, ab_ref)
        softmax(hh, jnp.where(kidx <= qidx, sa_ref[hh], -jnp.inf), pa_ref, aa_ref)
        pv(hh, 2 * qi, pa_ref, aa_ref)
        softmax(hh, jnp.where(kidx + hk <= qidx, sb_ref[hh], -jnp.inf), pb_ref, ab_ref)
        pv(hh, 2 * qi + 1, pb_ref, ab_ref)
        o_ref[0, :, hh * dh:(hh + 1) * dh] = (acc_ref[hh] / l_ref[hh]).T.astype(o_ref.dtype)


def _fox_prompt(q, k, v, c, *, n_heads, hpb):
    b, seq, width = q.shape
    dh = width // n_heads
    hk = 2 * LANES
    tq = 2 * hk
    groups = n_heads // hpb
    cg = c.reshape(b * groups, hpb, seq)
    wide = hpb * dh
    return pl.pallas_call(
        functools.partial(_fox_prompt_kernel, hk=hk, hpb=hpb),
        grid=(b, groups, seq // tq),
        in_specs=[pl.BlockSpec((1, tq, wide), lambda bi, h, i: (bi, i, h)),
                  pl.BlockSpec((1, seq, wide), lambda bi, h, i: (bi, 0, h)),
                  pl.BlockSpec((1, seq, wide), lambda bi, h, i: (bi, 0, h)),
                  pl.BlockSpec((1, hpb, seq), lambda bi, h, i: (bi * groups + h, 0, 0))],
        out_specs=pl.BlockSpec((1, tq, wide), lambda bi, h, i: (bi, i, h)),
        out_shape=jax.ShapeDtypeStruct((b, seq, width), BF16),
        scratch_shapes=[pltpu.VMEM((hpb, seq, 2 * dh), BF16), pltpu.VMEM((hpb, dh, seq), BF16),
                        pltpu.VMEM((hpb, 2 * dh, tq), BF16),
                        pltpu.VMEM((hpb, hk, tq), F32), pltpu.VMEM((hpb, hk, tq), F32),
                        pltpu.VMEM((hpb, hk, tq), BF16), pltpu.VMEM((hpb, hk, tq), BF16),
                        pltpu.VMEM((hpb, 1, tq), F32), pltpu.VMEM((hpb, 1, tq), F32),
                        pltpu.VMEM((hpb, 1, tq), F32), pltpu.VMEM((hpb, 1, tq), F32),
                        pltpu.VMEM((hpb, dh, tq), F32)],
        compiler_params=_params("parallel", "parallel", "arbitrary"), name="fox_prompt",
    )(q, k, v, cg)


def _fox_sample_kernel(q_ref, kn_ref, vn_ref, kc_ref, vc_ref, c_ref, o_ref, m_ref, l_ref, acc_ref, *, n_heads):
    ci = pl.program_id(1)
    dh = kc_ref.shape[1]
    tk = kc_ref.shape[0] // n_heads
    t_new = q_ref.shape[1]
    past = pl.num_programs(1) * tk

    def update(h, s, v):
        m = m_ref[h]
        m_new = jnp.maximum(m, jnp.max(s, axis=-1, keepdims=True))
        p = jnp.exp2(s - m_new)
        a = jnp.exp2(m - m_new)
        m_ref[h] = m_new
        l_ref[h] = a * l_ref[h] + jnp.sum(p, axis=-1, keepdims=True)
        acc_ref[h] = a * acc_ref[h] + _dot(p.astype(BF16), v)

    @pl.when(ci == 0)
    def _():
        r = lax.broadcasted_iota(jnp.int32, (t_new, t_new), 0)
        c = lax.broadcasted_iota(jnp.int32, (t_new, t_new), 1)
        for h in range(n_heads):
            hc = slice(h * dh, (h + 1) * dh)
            m_ref[h] = jnp.full((t_new, 1), -jnp.inf, F32)
            l_ref[h] = jnp.zeros((t_new, 1), F32)
            acc_ref[h] = jnp.zeros((t_new, dh), F32)
            s = _dot_nt(q_ref[0, :, hc], kn_ref[0, :, hc].astype(BF16))
            s = s - c_ref[0, h:h + 1, past:past + t_new] * LOG2E
            update(h, jnp.where(c <= r, s, -jnp.inf), vn_ref[0, :, hc].astype(BF16))

    start = pl.multiple_of(ci * tk, tk)
    for h in range(n_heads):
        head_rows = pl.ds(h, tk, stride=n_heads)
        s = _dot_nt(q_ref[0, :, h * dh:(h + 1) * dh], kc_ref[head_rows, :].astype(BF16))
        update(h, s - c_ref[0, h:h + 1, pl.ds(start, tk)] * LOG2E, vc_ref[head_rows, :].astype(BF16))

    @pl.when(ci == pl.num_programs(1) - 1)
    def _():
        for h in range(n_heads):
            o_ref[0, :, h * dh:(h + 1) * dh] = (acc_ref[h] / l_ref[h]).astype(o_ref.dtype)


def _fox_sample(q, kn, vn, kc, vc, c, *, tk):
    b, t_new, width = q.shape
    dh = kc.shape[1]
    n_heads = width // dh
    past = kc.shape[0] // (b * n_heads)
    chunks = past // tk
    new = pl.BlockSpec((1, t_new, width), lambda bi, ci: (bi, 0, 0))
    old = pl.BlockSpec((tk * n_heads, dh), lambda bi, ci: (bi * chunks + ci, 0))
    return pl.pallas_call(
        functools.partial(_fox_sample_kernel, n_heads=n_heads), grid=(b, chunks),
        in_specs=[new, new, new, old, old,
                  pl.BlockSpec((1, n_heads, c.shape[2]), lambda bi, ci: (bi, 0, 0))],
        out_specs=new,
        out_shape=jax.ShapeDtypeStruct((b, t_new, width), BF16),
        scratch_shapes=[pltpu.VMEM((n_heads, t_new, 1), F32), pltpu.VMEM((n_heads, t_new, 1), F32),
                        pltpu.VMEM((n_heads, t_new, dh), F32)],
        compiler_params=_params("parallel", "arbitrary"), name="fox_sample",
    )(q, kn, vn, kc, vc, c)


def _out_proj_kernel(fo_ref, go_ref, x_ref, w_ref, g_ref, b_ref, o_ref, *, alpha):
    half = fo_ref.shape[1]
    mix = _dot(fo_ref[...], w_ref[0:half, :]) + _dot(go_ref[...], w_ref[half:2 * half, :])
    o_ref[...] = _layer_norm(alpha * x_ref[...] + mix, g_ref[...], b_ref[...])


def _out_proj(fo, go, x, w, g, b, *, tm, alpha):
    m, d = x.shape
    row = lambda wd: pl.BlockSpec((tm, wd), lambda i: (i, 0))
    return pl.pallas_call(
        functools.partial(_out_proj_kernel, alpha=alpha), grid=(m // tm,),
        in_specs=[row(fo.shape[1]), row(go.shape[1]), row(d),
                  _resident(w.shape), _resident(g.shape), _resident(b.shape)],
        out_specs=row(d), out_shape=jax.ShapeDtypeStruct((m, d), F32),
        compiler_params=_params("parallel"), name="out_proj_ln",
    )(fo, go, x, w, g, b)


def _proj_kernel(x_ref, w_ref, o_ref, ob_ref):
    y = _dot(x_ref[...].astype(BF16), w_ref[...])
    o_ref[...] = y
    ob_ref[...] = y.astype(BF16)


def _proj(x, w, *, tm, tn):
    m, kd = x.shape
    n = w.shape[1]
    out = pl.BlockSpec((tm, tn), lambda i, j: (i, j))
    return pl.pallas_call(
        _proj_kernel, grid=(m // tm, n // tn),
        in_specs=[pl.BlockSpec((tm, kd), lambda i, j: (i, 0)),
                  pl.BlockSpec((kd, tn), lambda i, j: (0, j))],
        out_specs=[out, out],
        out_shape=[jax.ShapeDtypeStruct((m, n), F32), jax.ShapeDtypeStruct((m, n), BF16)],
        compiler_params=_params("parallel", "parallel"), name="mem_proj",
    )(x, w)


def _mem_attn_kernel(h_ref, wq_ref, wo_ref, mk_ref, mv_ref, g_ref, b_ref, o_ref, att_ref,
                     *, n_heads, alpha, scale):
    nb = mk_ref.shape[0]
    tm, d = h_ref.shape
    rb = tm // nb
    dh = d // n_heads
    h = h_ref[...]
    qb = (_dot(h.astype(BF16), wq_ref[...]) * scale).astype(BF16)
    for bi in range(nb):
        rows = slice(bi * rb, (bi + 1) * rb)
        for hd in range(n_heads):
            cols = slice(hd * dh, (hd + 1) * dh)
            s = _dot_nt(qb[rows, cols], mk_ref[bi, :, cols])
            p = jnp.exp(s - jnp.max(s, axis=-1, keepdims=True))
            l = jnp.sum(p, axis=-1, keepdims=True)
            o = _dot(p.astype(BF16), mv_ref[bi, :, cols])
            att_ref[rows, cols] = (o / l).astype(BF16)
    y = _dot(att_ref[...], wo_ref[...])
    o_ref[...] = _layer_norm(alpha * h + y, g_ref[...], b_ref[...])


def _mem_attn(h, wq, wo, mk, mv, g, b, *, tm, rows_per_batch, n_heads, alpha, scale):
    m, d = h.shape
    n_mem = mk.shape[1]
    if tm <= rows_per_batch:
        nb = 1
        steps_per_batch = rows_per_batch // tm
        mem_idx = lambda i: (i // steps_per_batch, 0, 0)
    else:
        nb = tm // rows_per_batch
        mem_idx = lambda i: (i, 0, 0)
    row = pl.BlockSpec((tm, d), lambda i: (i, 0))
    mem = pl.BlockSpec((nb, n_mem, d), mem_idx)
    return pl.pallas_call(
        functools.partial(_mem_attn_kernel, n_heads=n_heads, alpha=alpha, scale=scale),
        grid=(m // tm,),
        in_specs=[row, _resident(wq.shape), _resident(wo.shape), mem, mem,
                  _resident(g.shape), _resident(b.shape)],
        out_specs=row, out_shape=jax.ShapeDtypeStruct((m, d), F32),
        scratch_shapes=[pltpu.VMEM((tm, d), BF16)],
        compiler_params=_params("parallel"), name="mem_attn_ln",
    )(h, wq, wo, mk, mv, g, b)


def _ffn_kernel(x_ref, wu_ref, wd_ref, g_ref, b_ref, o_ref, xb_ref, *, alpha):
    f = pl.program_id(1)

    @pl.when(f == 0)
    def _():
        xb_ref[...] = x_ref[...].astype(BF16)
        o_ref[...] = jnp.zeros_like(o_ref)

    a = jnp.maximum(_dot(xb_ref[...], wu_ref[...]), 0.0)
    ab = (a * a).astype(BF16)
    slab = 512
    for n0 in range(0, o_ref.shape[1], slab):
        o_ref[:, n0:n0 + slab] += _dot(ab, wd_ref[:, n0:n0 + slab])

    @pl.when(f == pl.num_programs(1) - 1)
    def _():
        o_ref[...] = _layer_norm(alpha * x_ref[...] + o_ref[...], g_ref[...], b_ref[...])


def _ffn(x, wu, wd, g, b, *, tm, tf, alpha):
    m, d = x.shape
    dff = wu.shape[1]
    row = pl.BlockSpec((tm, d), lambda i, f: (i, 0))
    return pl.pallas_call(
        functools.partial(_ffn_kernel, alpha=alpha), grid=(m // tm, dff // tf),
        in_specs=[row, pl.BlockSpec((d, tf), lambda i, f: (0, f)),
                  pl.BlockSpec((tf, d), lambda i, f: (f, 0)),
                  _resident(g.shape), _resident(b.shape)],
        out_specs=row, out_shape=jax.ShapeDtypeStruct((m, d), F32),
        scratch_shapes=[pltpu.VMEM((tm, d), BF16)],
        compiler_params=_params("parallel", "arbitrary"), name="ffn_ln",
    )(x, wu, wd, g, b)


def _pad_lanes(a):
    return jnp.pad(a, ((0, 0), (0, LANES - a.shape[1])))


def kernel(x_prompt, x_sample, mem_prompt, cache_fox_k, cache_fox_v, cache_fox_logf, cache_mem_k, cache_mem_v, w_in, b_f, sgu_ln_g, sgu_ln_b, w_s, b_s, w_out, ln1_g, ln1_b, w_mq, w_mk, w_mv, w_mo, ln2_g, ln2_b, w_up, w_down, ln3_g, ln3_b):
    depth = w_in.shape[0]
    assert depth == 1
    b, seq, d = x_prompt.shape
    bs_, t_new, _ = x_sample.shape
    past, n_heads, dh = cache_fox_k.shape[2:]
    n_mem, mem_heads, mem_dh = cache_mem_k.shape[2:]
    n_groups, chunk = w_s.shape[1], w_s.shape[2]
    fox_w = n_heads * dh
    gw = d - fox_w
    alpha = (2 * depth) ** 0.25
    l = 0

    wi = w_in[l]
    o3 = 3 * fox_w
    o4 = o3 + n_heads
    wqkv = wi[:, :o3].astype(BF16)
    wf = _pad_lanes(wi[:, o3:o4]).astype(BF16)
    bf = _pad_lanes(b_f[l][None, :])
    wug = wi[:, o4:].astype(BF16)
    lg, lb = sgu_ln_g[l][None, :], sgu_ln_b[l][None, :]
    ws_p, bs_p = w_s[l], b_s[l][:, :, None]
    reps = chunk // t_new
    ws_s = jnp.tile(w_s[l][:, :t_new, :t_new], (1, reps, reps))
    bs_s = jnp.tile(b_s[l][:, :t_new], (1, reps))[:, :, None]
    wo = w_out[l].astype(BF16)
    wmq, wmk, wmv, wmo = (w[l].astype(BF16) for w in (w_mq, w_mk, w_mv, w_mo))
    wu, wd = w_up[l].astype(BF16), w_down[l].astype(BF16)
    ln = [a[l][None, :] for a in (ln1_g, ln1_b, ln2_g, ln2_b, ln3_g, ln3_b)]

    def post(h, fo, go, mk, mv, *, rows_per_batch, tm_attn):
        h = _out_proj(fo, go, h, wo, ln[0], ln[1], tm=512, alpha=alpha)
        h = _mem_attn(h, wmq, wmo, mk, mv, ln[2], ln[3], tm=tm_attn, rows_per_batch=rows_per_batch,
                      n_heads=mem_heads, alpha=alpha, scale=mem_dh ** -0.5)
        return _ffn(h, wu, wd, ln[4], ln[5], tm=512, tf=1024, alpha=alpha)

    xp = x_prompt.reshape(b * seq, d)
    q, k, v, logf, go = _in_proj(xp, wqkv, wf, bf, wug, lg, lb, ws_p, bs_p, tm=512, period=chunk,
                                 n_heads=n_heads, q_scale=dh ** -0.5 * LOG2E, emit_g=False)
    lt = jnp.transpose(logf.reshape(b, seq, n_heads), (0, 2, 1)).reshape(b * n_heads, seq // LANES, LANES)
    c = _cumsum_rows(lt, per_step=n_heads)
    fo = _fox_prompt(q.reshape(b, seq, fox_w), k.reshape(b, seq, fox_w), v.reshape(b, seq, fox_w), c,
                     n_heads=n_heads, hpb=2)
    mem2 = mem_prompt.reshape(b * n_mem, d)
    mk, mkb = _proj(mem2, wmk, tm=512, tn=1024)
    mv, mvb = _proj(mem2, wmv, tm=512, tn=1024)
    yp = post(xp, fo.reshape(b * seq, fox_w), go, mkb.reshape(b, n_mem, d), mvb.reshape(b, n_mem, d),
              rows_per_batch=seq, tm_attn=512)

    xs = x_sample.reshape(bs_ * t_new, d)
    qs, ks, vs, logfs, gos, gs = _in_proj(xs, wqkv, wf, bf, wug, lg, lb, ws_s, bs_s, tm=bs_ * t_new,
                                          period=t_new, n_heads=n_heads, q_scale=dh ** -0.5 * LOG2E,
                                          emit_g=True)
    rows_c = -(-(past + t_new) // LANES)
    rows_c = -(-rows_c // 8) * 8
    lcat = jnp.concatenate([cache_fox_logf[l], logfs.reshape(bs_, t_new, n_heads)], axis=1)
    lcat = jnp.pad(lcat, ((0, 0), (0, rows_c * LANES - past - t_new), (0, 0)))
    lts = jnp.transpose(lcat, (0, 2, 1)).reshape(bs_ * n_heads, rows_c, LANES)
    cs = _cumsum_rows(lts, per_step=n_heads).reshape(bs_, n_heads, rows_c * LANES)
    fos = _fox_sample(qs.reshape(bs_, t_new, fox_w), ks.reshape(bs_, t_new, fox_w), vs.reshape(bs_, t_new, fox_w),
                      cache_fox_k.reshape(bs_ * past * n_heads, dh), cache_fox_v.reshape(bs_ * past * n_heads, dh),
                      cs, tk=1024)
    cmk = cache_mem_k[l].reshape(bs_, n_mem, d).astype(BF16)
    cmv = cache_mem_v[l].reshape(bs_, n_mem, d).astype(BF16)
    ys = post(xs, fos.reshape(bs_ * t_new, fox_w), gos, cmk, cmv, rows_per_batch=t_new, tm_attn=128)

    return (yp.reshape(b, seq, d), ys.reshape(bs_, t_new, d),
            k.reshape(1, b, seq, n_heads, dh), v.reshape(1, b, seq, n_heads, dh),
            logf.reshape(1, b, seq, n_heads),
            mk.reshape(1, b, n_mem, mem_heads, mem_dh), mv.reshape(1, b, n_mem, mem_heads, mem_dh),
            ks.reshape(1, bs_, t_new, n_heads, dh), vs.reshape(1, bs_, t_new, n_heads, dh),
            logfs.reshape(1, bs_, t_new, n_heads), gs.reshape(1, bs_, t_new, gw))
```

```python
import functools
import math

import jax
import jax.numpy as jnp
from jax import lax
from jax.experimental import pallas as pl
from jax.experimental.pallas import tpu as pltpu

F32 = jnp.float32
BF16 = jnp.bfloat16

LN_EPS = 1e-5
LANES = 128
GELU_C = math.sqrt(2.0 / math.pi)
LOG2E = 1.0 / math.log(2.0)
VMEM_LIMIT = 60 * 1024 * 1024


def _dot(a, b):
    return jnp.dot(a, b, preferred_element_type=F32)


def _dot_nt(a, b):
    return lax.dot_general(a, b, (((1,), (1,)), ((), ())), preferred_element_type=F32)


def _gelu(x):
    return 0.5 * x * (1.0 + jnp.tanh(GELU_C * (x + 0.044715 * (x * x * x))))


def _layer_norm(x, g, b):
    mu = jnp.mean(x, axis=-1, keepdims=True)
    xc = x - mu
    var = jnp.mean(xc * xc, axis=-1, keepdims=True)
    return xc * lax.rsqrt(var + LN_EPS) * g + b


def _log_sigmoid(x):
    return jnp.minimum(x, 0.0) - jnp.log1p(jnp.exp(-jnp.abs(x)))


def _resident(shape):
    nd = len(shape)
    return pl.BlockSpec(shape, lambda *_: (0,) * nd, pipeline_mode=pl.Buffered(1))


def _params(*sem):
    return pltpu.CompilerParams(dimension_semantics=sem, vmem_limit_bytes=VMEM_LIMIT)


def _in_proj_kernel(x_ref, wqkv_ref, wf_ref, bf_ref, wug_ref, lg_ref, lb_ref, ws_ref, bs_ref,
                    q_ref, k_ref, v_ref, logf_ref, go_ref, *g_ref,
                    fox_w, n_heads, gw, n_groups, period, q_scale):
    tm = x_ref.shape[0]
    chunk = ws_ref.shape[1]
    gd = gw // n_groups
    xb = x_ref[...].astype(BF16)
    q_ref[...] = (_dot(xb, wqkv_ref[:, 0:fox_w]) * q_scale).astype(BF16)
    k_ref[...] = _dot(xb, wqkv_ref[:, fox_w:2 * fox_w])
    v_ref[...] = _dot(xb, wqkv_ref[:, 2 * fox_w:3 * fox_w])
    zf = _dot(xb, wf_ref[...]) + bf_ref[...]
    logf_ref[...] = _log_sigmoid(zf)[:, :n_heads]
    g = _layer_norm(_gelu(_dot(xb, wug_ref[:, gw:2 * gw])), lg_ref[...], lb_ref[...])
    if g_ref:
        g_ref[0][...] = g
    gb = g.astype(BF16)
    u = _gelu(_dot(xb, wug_ref[:, 0:gw]))
    r = lax.broadcasted_iota(jnp.int32, (chunk, chunk), 0)
    c = lax.broadcasted_iota(jnp.int32, (chunk, chunk), 1)
    sh = period.bit_length() - 1
    keep = ((r >> sh) == (c >> sh)) & ((c & (period - 1)) <= (r & (period - 1)))
    for gi in range(n_groups):
        wsg = jnp.where(keep, ws_ref[gi], 0.0).astype(BF16)
        cols = slice(gi * gd, (gi + 1) * gd)
        for ci in range(tm // chunk):
            rows = slice(ci * chunk, (ci + 1) * chunk)
            s = _dot(wsg, gb[rows, cols]) + bs_ref[gi]
            go_ref[rows, cols] = (u[rows, cols] * s).astype(BF16)


def _in_proj(x, wqkv, wf, bf, wug, lg, lb, ws, bs, *, tm, period, n_heads, q_scale, emit_g):
    m, d = x.shape
    fox_w = wqkv.shape[1] // 3
    gw = wug.shape[1] // 2
    n_groups = ws.shape[0]
    row = lambda w: pl.BlockSpec((tm, w), lambda i: (i, 0))
    out_shape = [jax.ShapeDtypeStruct((m, fox_w), BF16),
                 jax.ShapeDtypeStruct((m, fox_w), F32),
                 jax.ShapeDtypeStruct((m, fox_w), F32),
                 jax.ShapeDtypeStruct((m, n_heads), F32),
                 jax.ShapeDtypeStruct((m, gw), BF16)]
    out_specs = [row(fox_w), row(fox_w), row(fox_w), row(n_heads), row(gw)]
    if emit_g:
        out_shape.append(jax.ShapeDtypeStruct((m, gw), F32))
        out_specs.append(row(gw))
    kern = functools.partial(_in_proj_kernel, fox_w=fox_w, n_heads=n_heads, gw=gw,
                             n_groups=n_groups, period=period, q_scale=q_scale)
    return pl.pallas_call(
        kern, grid=(m // tm,),
        in_specs=[row(d), _resident(wqkv.shape), _resident(wf.shape), _resident(bf.shape),
                  _resident(wug.shape), _resident(lg.shape), _resident(lb.shape),
                  _resident(ws.shape), _resident(bs.shape)],
        out_specs=out_specs, out_shape=out_shape,
        compiler_params=_params("parallel"), name="in_proj_sgu",
    )(x, wqkv, wf, bf, wug, lg, lb, ws, bs)


def _cumsum_kernel(x_ref, o_ref, *, rows):
    x = x_ref[...]
    n = x.shape[0]
    hi = lax.Precision.HIGHEST
    ii = lax.broadcasted_iota(jnp.int32, (LANES, LANES), 0)
    jj = lax.broadcasted_iota(jnp.int32, (LANES, LANES), 1)
    within = jnp.dot(x, (ii <= jj).astype(F32), precision=hi, preferred_element_type=F32)
    totals = jnp.dot(x, jnp.ones((LANES, LANES), F32), precision=hi, preferred_element_type=F32)
    ri = lax.broadcasted_iota(jnp.int32, (n, n), 0)
    rj = lax.broadcasted_iota(jnp.int32, (n, n), 1)
    seq_i = jnp.zeros((n, n), jnp.int32)
    seq_j = jnp.zeros((n, n), jnp.int32)
    for s in range(rows, n, rows):
        seq_i += (ri >= s).astype(jnp.int32)
        seq_j += (rj >= s).astype(jnp.int32)
    earlier = ((seq_i == seq_j) & (rj < ri)).astype(F32)
    o_ref[...] = within + jnp.dot(earlier, totals, precision=hi, preferred_element_type=F32)


def _cumsum_rows(x, *, per_step):
    n, rows, _ = x.shape
    x2 = x.reshape(n * rows, LANES)
    spec = pl.BlockSpec((per_step * rows, LANES), lambda i: (i, 0))
    out = pl.pallas_call(
        functools.partial(_cumsum_kernel, rows=rows), grid=(n // per_step,),
        in_specs=[spec], out_specs=spec,
        out_shape=jax.ShapeDtypeStruct(x2.shape, F32),
        compiler_params=_params("parallel"), name="logf_cumsum",
    )(x2)
    return out.reshape(n, rows * LANES)


def _fox_prompt_kernel(q_ref, k_ref, v_ref, c_ref, mask_ref, o_ref, ka_ref, vt_ref, qa_ref, sa_ref, sb_ref,
                       pa_ref, pb_ref, aa_ref, ab_ref, m_ref, l_ref, acc_ref, *, hk, hpb):
    qi = pl.program_id(2)
    tq = q_ref.shape[1]
    seq = k_ref.shape[1]
    dh = vt_ref.shape[1]
    prep = 512

    @pl.when(qi == 0)
    def _():
        row = lax.broadcasted_iota(jnp.int32, (dh, prep), 0)
        for hh in range(hpb):
            hc = slice(hh * dh, (hh + 1) * dh)
            for r0 in range(0, seq, prep):
                rs = slice(r0, r0 + prep)
                ka_ref[hh, rs, 0:dh] = k_ref[0, rs, hc].astype(BF16)
                vt_ref[hh, :, rs] = v_ref[0, rs, hc].T.astype(BF16)
                c2 = c_ref[0, hh:hh + 1, rs] * LOG2E
                hi = c2.astype(BF16).astype(F32)
                r1 = c2 - hi
                mid = r1.astype(BF16).astype(F32)
                lo = r1 - mid
                aug = jnp.where(row == 0, -hi, jnp.where(row == 1, -mid, jnp.where(row == 2, -lo, 0.0)))
                ka_ref[hh, rs, dh:2 * dh] = aug.T.astype(BF16)

    ones = (lax.broadcasted_iota(jnp.int32, (dh, tq), 0) < 3).astype(BF16)
    for hh in range(hpb):
        qa_ref[hh, 0:dh, :] = q_ref[0, :, hh * dh:(hh + 1) * dh].astype(F32).T.astype(BF16)
        qa_ref[hh, dh:2 * dh, :] = ones

    def qk(hh, blk):
        start = pl.multiple_of(blk * hk, hk)
        return _dot(ka_ref[hh, pl.ds(start, hk), :], qa_ref[hh])

    def softmax(hh, s, p_ref, a_ref):
        m = m_ref[hh]
        m_new = jnp.maximum(m, jnp.max(s, axis=0, keepdims=True))
        p = jnp.exp2(s - m_new)
        a = jnp.exp2(m - m_new)
        m_ref[hh] = m_new
        l_ref[hh] = a * l_ref[hh] + jnp.sum(p, axis=0, keepdims=True)
        p_ref[hh] = p.astype(BF16)
        a_ref[hh] = a

    def pv(hh, blk, p_ref, a_ref):
        start = pl.multiple_of(blk * hk, hk)
        acc_ref[hh] = a_ref[hh] * acc_ref[hh] + _dot(vt_ref[hh, :, pl.ds(start, hk)], p_ref[hh])

    for hh in range(hpb):
        m_ref[hh] = jnp.full((1, tq), -jnp.inf, F32)
        l_ref[hh] = jnp.zeros((1, tq), F32)
        acc_ref[hh] = jnp.zeros((dh, tq), F32)
        pb_ref[hh] = jnp.zeros((hk, tq), BF16)
        ab_ref[hh] = jnp.ones((1, tq), F32)
        sa_ref[hh] = qk(hh, 0)

    def body(j, carry):
        for hh in range(hpb):
            sb_ref[hh] = qk(hh, 2 * j + 1)
            pv(hh, jnp.maximum(2 * j - 1, 0), pb_ref, ab_ref)
            softmax(hh, sa_ref[hh], pa_ref, aa_ref)
            sa_ref[hh] = qk(hh, 2 * j + 2)
            pv(hh, 2 * j, pa_ref, aa_ref)
            softmax(hh, sb_ref[hh], pb_ref, ab_ref)
        return carry

    lax.fori_loop(0, qi, body, 0)
    for hh in range(hpb):
        sb_ref[hh] = qk(hh, 2 * qi + 1)
        pv(hh, jnp.maximum(2 * qi - 1, 0), pb_ref, ab_ref)
        softmax(hh, sa_ref[hh] + mask_ref[0], pa_ref, aa_ref)
        pv(hh, 2 * qi, pa_ref, aa_ref)
        softmax(hh, sb_ref[hh] + mask_ref[1], pb_ref, ab_ref)
        pv(hh, 2 * qi + 1, pb_ref, ab_ref)
        o_ref[0, :, hh * dh:(hh + 1) * dh] = (acc_ref[hh] / l_ref[hh]).T.astype(o_ref.dtype)


def _fox_prompt(q, k, v, c, *, n_heads, hpb):
    b, seq, width = q.shape
    dh = width // n_heads
    hk = 2 * LANES
    tq = 2 * hk
    groups = n_heads // hpb
    cg = c.reshape(b * groups, hpb, seq)
    wide = hpb * dh
    key = jnp.arange(2 * hk, dtype=jnp.int32).reshape(2, hk, 1)
    mask = jnp.where(key <= jnp.arange(tq, dtype=jnp.int32)[None, None, :], 0.0, -jnp.inf).astype(F32)
    return pl.pallas_call(
        functools.partial(_fox_prompt_kernel, hk=hk, hpb=hpb),
        grid=(b, groups, seq // tq),
        in_specs=[pl.BlockSpec((1, tq, wide), lambda bi, h, i: (bi, i, h)),
                  pl.BlockSpec((1, seq, wide), lambda bi, h, i: (bi, 0, h)),
                  pl.BlockSpec((1, seq, wide), lambda bi, h, i: (bi, 0, h)),
                  pl.BlockSpec((1, hpb, seq), lambda bi, h, i: (bi * groups + h, 0, 0)),
                  _resident(mask.shape)],
        out_specs=pl.BlockSpec((1, tq, wide), lambda bi, h, i: (bi, i, h)),
        out_shape=jax.ShapeDtypeStruct((b, seq, width), BF16),
        scratch_shapes=[pltpu.VMEM((hpb, seq, 2 * dh), BF16), pltpu.VMEM((hpb, dh, seq), BF16),
                        pltpu.VMEM((hpb, 2 * dh, tq), BF16),
                        pltpu.VMEM((hpb, hk, tq), F32), pltpu.VMEM((hpb, hk, tq), F32),
                        pltpu.VMEM((hpb, hk, tq), BF16), pltpu.VMEM((hpb, hk, tq), BF16),
                        pltpu.VMEM((hpb, 1, tq), F32), pltpu.VMEM((hpb, 1, tq), F32),
                        pltpu.VMEM((hpb, 1, tq), F32), pltpu.VMEM((hpb, 1, tq), F32),
                        pltpu.VMEM((hpb, dh, tq), F32)],
        compiler_params=_params("parallel", "parallel", "arbitrary"), name="fox_prompt",
    )(q, k, v, cg, mask)


def _fox_sample_kernel(q_ref, kn_ref, vn_ref, kc_ref, vc_ref, c_ref, o_ref, m_ref, l_ref, acc_ref, *, n_heads):
    ci = pl.program_id(1)
    dh = kc_ref.shape[1]
    tk = kc_ref.shape[0] // n_heads
    t_new = q_ref.shape[1]
    past = pl.num_programs(1) * tk

    def update(h, s, v):
        m = m_ref[h]
        m_new = jnp.maximum(m, jnp.max(s, axis=-1, keepdims=True))
        p = jnp.exp2(s - m_new)
        a = jnp.exp2(m - m_new)
        m_ref[h] = m_new
        l_ref[h] = a * l_ref[h] + jnp.sum(p, axis=-1, keepdims=True)
        acc_ref[h] = a * acc_ref[h] + _dot(p.astype(BF16), v)

    @pl.when(ci == 0)
    def _():
        r = lax.broadcasted_iota(jnp.int32, (t_new, t_new), 0)
        c = lax.broadcasted_iota(jnp.int32, (t_new, t_new), 1)
        for h in range(n_heads):
            hc = slice(h * dh, (h + 1) * dh)
            m_ref[h] = jnp.full((t_new, 1), -jnp.inf, F32)
            l_ref[h] = jnp.zeros((t_new, 1), F32)
            acc_ref[h] = jnp.zeros((t_new, dh), F32)
            s = _dot_nt(q_ref[0, :, hc], kn_ref[0, :, hc].astype(BF16))
            s = s - c_ref[0, h:h + 1, past:past + t_new] * LOG2E
            update(h, jnp.where(c <= r, s, -jnp.inf), vn_ref[0, :, hc].astype(BF16))

    start = pl.multiple_of(ci * tk, tk)
    for h in range(n_heads):
        head_rows = pl.ds(h, tk, stride=n_heads)
        s = _dot_nt(q_ref[0, :, h * dh:(h + 1) * dh], kc_ref[head_rows, :].astype(BF16))
        update(h, s - c_ref[0, h:h + 1, pl.ds(start, tk)] * LOG2E, vc_ref[head_rows, :].astype(BF16))

    @pl.when(ci == pl.num_programs(1) - 1)
    def _():
        for h in range(n_heads):
            o_ref[0, :, h * dh:(h + 1) * dh] = (acc_ref[h] / l_ref[h]).astype(o_ref.dtype)


def _fox_sample(q, kn, vn, kc, vc, c, *, tk):
    b, t_new, width = q.shape
    dh = kc.shape[1]
    n_heads = width // dh
    past = kc.shape[0] // (b * n_heads)
    chunks = past // tk
    new = pl.BlockSpec((1, t_new, width), lambda bi, ci: (bi, 0, 0))
    old = pl.BlockSpec((tk * n_heads, dh), lambda bi, ci: (bi * chunks + ci, 0))
    return pl.pallas_call(
        functools.partial(_fox_sample_kernel, n_heads=n_heads), grid=(b, chunks),
        in_specs=[new, new, new, old, old,
                  pl.BlockSpec((1, n_heads, c.shape[2]), lambda bi, ci: (bi, 0, 0))],
        out_specs=new,
        out_shape=jax.ShapeDtypeStruct((b, t_new, width), BF16),
        scratch_shapes=[pltpu.VMEM((n_heads, t_new, 1), F32), pltpu.VMEM((n_heads, t_new, 1), F32),
                        pltpu.VMEM((n_heads, t_new, dh), F32)],
        compiler_params=_params("parallel", "arbitrary"), name="fox_sample",
    )(q, kn, vn, kc, vc, c)


def _out_proj_kernel(fo_ref, go_ref, x_ref, w_ref, g_ref, b_ref, o_ref, *, alpha, splits):
    half = fo_ref.shape[1]
    sub = x_ref.shape[0] // splits
    for r0 in range(0, x_ref.shape[0], sub):
        rows = slice(r0, r0 + sub)
        mix = _dot(fo_ref[rows, :], w_ref[0:half, :]) + _dot(go_ref[rows, :], w_ref[half:2 * half, :])
        o_ref[rows, :] = _layer_norm(alpha * x_ref[rows, :] + mix, g_ref[...], b_ref[...])


def _out_proj(fo, go, x, w, g, b, *, tm, alpha, splits):
    m, d = x.shape
    row = lambda wd: pl.BlockSpec((tm, wd), lambda i: (i, 0))
    return pl.pallas_call(
        functools.partial(_out_proj_kernel, alpha=alpha, splits=splits), grid=(m // tm,),
        in_specs=[row(fo.shape[1]), row(go.shape[1]), row(d),
                  _resident(w.shape), _resident(g.shape), _resident(b.shape)],
        out_specs=row(d), out_shape=jax.ShapeDtypeStruct((m, d), F32),
        compiler_params=_params("parallel"), name="out_proj_ln",
    )(fo, go, x, w, g, b)


def _mem_proj_kernel(x_ref, wk_ref, wv_ref, kt_ref, vt_ref, kb_ref, vb_ref, *, n_heads):
    tm, d = x_ref.shape
    lane_tiles = d // n_heads // LANES
    stride = lane_tiles * n_heads
    xb = x_ref[...].astype(BF16)
    for w_ref, t_ref, b_ref in ((wk_ref, kt_ref, kb_ref), (wv_ref, vt_ref, vb_ref)):
        y = _dot(xb, w_ref[...])
        b_ref[...] = y.astype(BF16)
        for hd in range(n_heads):
            for j in range(lane_tiles):
                c0 = (hd * lane_tiles + j) * LANES
                t_ref[pl.ds(j * n_heads + hd, tm, stride=stride), :] = y[:, c0:c0 + LANES]


def _mem_proj(x, wk, wv, *, tm, n_heads):
    m, d = x.shape
    rows = pl.BlockSpec((tm, d), lambda i: (i, 0))
    tiled = pl.BlockSpec((tm * d // LANES, LANES), lambda i: (i, 0))
    return pl.pallas_call(
        functools.partial(_mem_proj_kernel, n_heads=n_heads), grid=(m // tm,),
        in_specs=[rows, _resident(wk.shape), _resident(wv.shape)],
        out_specs=[tiled, tiled, rows, rows],
        out_shape=[jax.ShapeDtypeStruct((m * d // LANES, LANES), F32)] * 2
        + [jax.ShapeDtypeStruct((m, d), BF16)] * 2,
        compiler_params=_params("parallel"), name="mem_proj",
    )(x, wk, wv)


def _mem_attn_kernel(h_ref, wq_ref, wo_ref, mk_ref, mv_ref, g_ref, b_ref, o_ref, att_ref,
                     *, n_heads, n_mem, nb, alpha, scale, splits):
    tm, d = h_ref.shape
    dh = d // n_heads
    lane_tiles = dh // LANES

    def mem_head(ref, bi, hd):
        if len(ref.shape) == 3:
            return ref[bi, :, hd * dh:(hd + 1) * dh].astype(BF16)
        stride = lane_tiles * n_heads
        parts = [ref[pl.ds(bi * n_mem * stride + j * n_heads + hd, n_mem, stride=stride), :]
                 for j in range(lane_tiles)]
        return jnp.concatenate(parts, axis=1).astype(BF16)

    groups = splits if nb == 1 else 1
    rg = tm // groups
    rb = rg // nb
    for gi in range(groups):
        rows = slice(gi * rg, (gi + 1) * rg)
        h = h_ref[rows, :]
        qb = (_dot(h.astype(BF16), wq_ref[...]) * scale).astype(BF16)
        for bi in range(nb):
            for hd in range(n_heads):
                cols = slice(hd * dh, (hd + 1) * dh)
                s = _dot_nt(qb[bi * rb:(bi + 1) * rb, cols], mem_head(mk_ref, bi, hd))
                p = jnp.exp(s - jnp.max(s, axis=-1, keepdims=True))
                l = jnp.sum(p, axis=-1, keepdims=True)
                o = _dot(p.astype(BF16), mem_head(mv_ref, bi, hd))
                att_ref[gi * rg + bi * rb:gi * rg + (bi + 1) * rb, cols] = (o / l).astype(BF16)
        y = _dot(att_ref[rows, :], wo_ref[...])
        o_ref[rows, :] = _layer_norm(alpha * h + y, g_ref[...], b_ref[...])


def _mem_attn(h, wq, wo, mk, mv, g, b, *, tm, rows_per_batch, n_heads, n_mem, alpha, scale, splits):
    m, d = h.shape
    if tm <= rows_per_batch:
        nb = 1
        steps_per_batch = rows_per_batch // tm
        batch_of = lambda i: i // steps_per_batch
    else:
        nb = tm // rows_per_batch
        batch_of = lambda i: i
    row = pl.BlockSpec((tm, d), lambda i: (i, 0))
    if mk.ndim == 3:
        mem = pl.BlockSpec((nb, n_mem, d), lambda i: (batch_of(i), 0, 0))
    else:
        mem = pl.BlockSpec((nb * n_mem * d // LANES, LANES), lambda i: (batch_of(i), 0))
    return pl.pallas_call(
        functools.partial(_mem_attn_kernel, n_heads=n_heads, n_mem=n_mem, nb=nb, alpha=alpha, scale=scale,
                          splits=splits),
        grid=(m // tm,),
        in_specs=[row, _resident(wq.shape), _resident(wo.shape), mem, mem,
                  _resident(g.shape), _resident(b.shape)],
        out_specs=row, out_shape=jax.ShapeDtypeStruct((m, d), F32),
        scratch_shapes=[pltpu.VMEM((tm, d), BF16)],
        compiler_params=_params("parallel"), name="mem_attn_ln",
    )(h, wq, wo, mk, mv, g, b)


def _ffn_kernel(x_ref, wu_ref, wd_ref, g_ref, b_ref, o_ref, xb_ref, *, alpha):
    f = pl.program_id(1)

    @pl.when(f == 0)
    def _():
        xb_ref[...] = x_ref[...].astype(BF16)
        o_ref[...] = jnp.zeros_like(o_ref)

    a = jnp.maximum(_dot(xb_ref[...], wu_ref[...]), 0.0)
    ab = (a * a).astype(BF16)
    slab = 512
    for n0 in range(0, o_ref.shape[1], slab):
        o_ref[:, n0:n0 + slab] += _dot(ab, wd_ref[:, n0:n0 + slab])

    @pl.when(f == pl.num_programs(1) - 1)
    def _():
        o_ref[...] = _layer_norm(alpha * x_ref[...] + o_ref[...], g_ref[...], b_ref[...])


def _ffn(x, wu, wd, g, b, *, tm, tf, alpha):
    m, d = x.shape
    dff = wu.shape[1]
    row = pl.BlockSpec((tm, d), lambda i, f: (i, 0))
    return pl.pallas_call(
        functools.partial(_ffn_kernel, alpha=alpha), grid=(m // tm, dff // tf),
        in_specs=[row, pl.BlockSpec((d, tf), lambda i, f: (0, f)),
                  pl.BlockSpec((tf, d), lambda i, f: (f, 0)),
                  _resident(g.shape), _resident(b.shape)],
        out_specs=row, out_shape=jax.ShapeDtypeStruct((m, d), F32),
        scratch_shapes=[pltpu.VMEM((tm, d), BF16)],
        compiler_params=_params("parallel", "arbitrary"), name="ffn_ln",
    )(x, wu, wd, g, b)


def _pad_lanes(a):
    return jnp.pad(a, ((0, 0), (0, LANES - a.shape[1])))


def kernel(x_prompt, x_sample, mem_prompt, cache_fox_k, cache_fox_v, cache_fox_logf, cache_mem_k, cache_mem_v, w_in, b_f, sgu_ln_g, sgu_ln_b, w_s, b_s, w_out, ln1_g, ln1_b, w_mq, w_mk, w_mv, w_mo, ln2_g, ln2_b, w_up, w_down, ln3_g, ln3_b):
    depth = w_in.shape[0]
    assert depth == 1
    b, seq, d = x_prompt.shape
    bs_, t_new, _ = x_sample.shape
    past, n_heads, dh = cache_fox_k.shape[2:]
    n_mem, mem_heads, mem_dh = cache_mem_k.shape[2:]
    n_groups, chunk = w_s.shape[1], w_s.shape[2]
    fox_w = n_heads * dh
    gw = d - fox_w
    alpha = (2 * depth) ** 0.25
    l = 0

    wi = w_in[l]
    o3 = 3 * fox_w
    o4 = o3 + n_heads
    wqkv = wi[:, :o3].astype(BF16)
    wf = _pad_lanes(wi[:, o3:o4]).astype(BF16)
    bf = _pad_lanes(b_f[l][None, :])
    wug = wi[:, o4:].astype(BF16)
    lg, lb = sgu_ln_g[l][None, :], sgu_ln_b[l][None, :]
    ws_p, bs_p = w_s[l], b_s[l][:, :, None]
    reps = chunk // t_new
    ws_s = jnp.tile(w_s[l][:, :t_new, :t_new], (1, reps, reps))
    bs_s = jnp.tile(b_s[l][:, :t_new], (1, reps))[:, :, None]
    wo = w_out[l].astype(BF16)
    wmq, wmk, wmv, wmo = (w[l].astype(BF16) for w in (w_mq, w_mk, w_mv, w_mo))
    wu, wd = w_up[l].astype(BF16), w_down[l].astype(BF16)
    ln = [a[l][None, :] for a in (ln1_g, ln1_b, ln2_g, ln2_b, ln3_g, ln3_b)]

    def post(h, fo, go, mk, mv, *, rows_per_batch, tm_attn):
        h = _out_proj(fo, go, h, wo, ln[0], ln[1], tm=512, alpha=alpha, splits=2)
        h = _mem_attn(h, wmq, wmo, mk, mv, ln[2], ln[3], tm=tm_attn, rows_per_batch=rows_per_batch,
                      n_heads=mem_heads, n_mem=n_mem, alpha=alpha, scale=mem_dh ** -0.5, splits=2)
        return _ffn(h, wu, wd, ln[4], ln[5], tm=512, tf=1024, alpha=alpha)

    xp = x_prompt.reshape(b * seq, d)
    q, k, v, logf, go = _in_proj(xp, wqkv, wf, bf, wug, lg, lb, ws_p, bs_p, tm=512, period=chunk,
                                 n_heads=n_heads, q_scale=dh ** -0.5 * LOG2E, emit_g=False)
    lt = jnp.transpose(logf.reshape(b, seq, n_heads), (0, 2, 1)).reshape(b * n_heads, seq // LANES, LANES)
    c = _cumsum_rows(lt, per_step=n_heads)
    fo = _fox_prompt(q.reshape(b, seq, fox_w), k.reshape(b, seq, fox_w), v.reshape(b, seq, fox_w), c,
                     n_heads=n_heads, hpb=4)
    mem2 = mem_prompt.reshape(b * n_mem, d)
    mkt, mvt, mkb, mvb = _mem_proj(mem2, wmk, wmv, tm=512, n_heads=mem_heads)

    def untile_rows(a):
        a = a.reshape(b, n_mem, mem_dh // LANES, mem_heads, LANES)
        return jnp.transpose(a, (0, 1, 3, 2, 4)).reshape(1, b, n_mem, mem_heads, mem_dh)

    mk, mv = untile_rows(mkt), untile_rows(mvt)
    yp = post(xp, fo.reshape(b * seq, fox_w), go, mkb.reshape(b, n_mem, d), mvb.reshape(b, n_mem, d),
              rows_per_batch=seq, tm_attn=512)

    xs = x_sample.reshape(bs_ * t_new, d)
    qs, ks, vs, logfs, gos, gs = _in_proj(xs, wqkv, wf, bf, wug, lg, lb, ws_s, bs_s, tm=bs_ * t_new,
                                          period=t_new, n_heads=n_heads, q_scale=dh ** -0.5 * LOG2E,
                                          emit_g=True)
    rows_c = -(-(past + t_new) // LANES)
    rows_c = -(-rows_c // 8) * 8
    lcat = jnp.concatenate([cache_fox_logf[l], logfs.reshape(bs_, t_new, n_heads)], axis=1)
    lcat = jnp.pad(lcat, ((0, 0), (0, rows_c * LANES - past - t_new), (0, 0)))
    lts = jnp.transpose(lcat, (0, 2, 1)).reshape(bs_ * n_heads, rows_c, LANES)
    cs = _cumsum_rows(lts, per_step=n_heads).reshape(bs_, n_heads, rows_c * LANES)
    fos = _fox_sample(qs.reshape(bs_, t_new, fox_w), ks.reshape(bs_, t_new, fox_w), vs.reshape(bs_, t_new, fox_w),
                      cache_fox_k.reshape(bs_ * past * n_heads, dh), cache_fox_v.reshape(bs_ * past * n_heads, dh),
                      cs, tk=1024)
    def tile_rows(a):
        a = a.reshape(bs_, n_mem, mem_heads, mem_dh // LANES, LANES)
        return jnp.transpose(a, (0, 1, 3, 2, 4)).reshape(bs_ * n_mem * d // LANES, LANES)

    ys = post(xs, fos.reshape(bs_ * t_new, fox_w), gos, tile_rows(cache_mem_k), tile_rows(cache_mem_v),
              rows_per_batch=t_new, tm_attn=64)

    return (yp.reshape(b, seq, d), ys.reshape(bs_, t_new, d),
            k.reshape(1, b, seq, n_heads, dh), v.reshape(1, b, seq, n_heads, dh),
            logf.reshape(1, b, seq, n_heads),
            mk, mv,
            ks.reshape(1, bs_, t_new, n_heads, dh), vs.reshape(1, bs_, t_new, n_heads, dh),
            logfs.reshape(1, bs_, t_new, n_heads), gs.reshape(1, bs_, t_new, gw))
```

```python
import functools
import math

import jax
import jax.numpy as jnp
from jax import lax
from jax.experimental import pallas as pl
from jax.experimental.pallas import tpu as pltpu

F32 = jnp.float32
BF16 = jnp.bfloat16

LN_EPS = 1e-5
LANES = 128
BF16_ROWS = 16
GELU_C = math.sqrt(2.0 / math.pi)
LOG2E = 1.0 / math.log(2.0)
VMEM_LIMIT = 60 * 1024 * 1024


def _dot(a, b):
    return jnp.dot(a, b, preferred_element_type=F32)


def _dot_nt(a, b):
    return lax.dot_general(a, b, (((1,), (1,)), ((), ())), preferred_element_type=F32)


def _gelu(x):
    return 0.5 * x * (1.0 + jnp.tanh(GELU_C * (x + 0.044715 * (x * x * x))))


def _layer_norm(x, g, b):
    mu = jnp.mean(x, axis=-1, keepdims=True)
    xc = x - mu
    var = jnp.mean(xc * xc, axis=-1, keepdims=True)
    return xc * lax.rsqrt(var + LN_EPS) * g + b


def _log_sigmoid(x):
    return jnp.minimum(x, 0.0) - jnp.log1p(jnp.exp(-jnp.abs(x)))


def _resident(shape):
    nd = len(shape)
    return pl.BlockSpec(shape, lambda *_: (0,) * nd, pipeline_mode=pl.Buffered(1))


def _params(*sem):
    return pltpu.CompilerParams(dimension_semantics=sem, vmem_limit_bytes=VMEM_LIMIT)


def _in_proj_kernel(x_ref, wqkv_ref, wf_ref, bf_ref, wug_ref, lg_ref, lb_ref, ws_ref, bs_ref,
                    q_ref, k_ref, v_ref, logf_ref, go_ref, *g_ref,
                    fox_w, n_heads, gw, n_groups, period, q_scale):
    tm = x_ref.shape[0]
    chunk = ws_ref.shape[1]
    gd = gw // n_groups
    xb = x_ref[...].astype(BF16)
    q_ref[...] = (_dot(xb, wqkv_ref[:, 0:fox_w]) * q_scale).astype(BF16)
    k_ref[...] = _dot(xb, wqkv_ref[:, fox_w:2 * fox_w])
    v_ref[...] = _dot(xb, wqkv_ref[:, 2 * fox_w:3 * fox_w])
    zf = _dot(xb, wf_ref[...]) + bf_ref[...]
    logf_ref[...] = _log_sigmoid(zf)[:, :n_heads]
    g = _layer_norm(_gelu(_dot(xb, wug_ref[:, gw:2 * gw])), lg_ref[...], lb_ref[...])
    if g_ref:
        g_ref[0][...] = g
    gb = g.astype(BF16)
    u = _gelu(_dot(xb, wug_ref[:, 0:gw]))
    r = lax.broadcasted_iota(jnp.int32, (chunk, chunk), 0)
    c = lax.broadcasted_iota(jnp.int32, (chunk, chunk), 1)
    sh = period.bit_length() - 1
    keep = ((r >> sh) == (c >> sh)) & ((c & (period - 1)) <= (r & (period - 1)))
    for gi in range(n_groups):
        wsg = jnp.where(keep, ws_ref[gi], 0.0).astype(BF16)
        cols = slice(gi * gd, (gi + 1) * gd)
        for ci in range(tm // chunk):
            rows = slice(ci * chunk, (ci + 1) * chunk)
            s = _dot(wsg, gb[rows, cols]) + bs_ref[gi]
            go_ref[rows, cols] = (u[rows, cols] * s).astype(BF16)


def _in_proj(x, wqkv, wf, bf, wug, lg, lb, ws, bs, *, tm, period, n_heads, q_scale, emit_g):
    m, d = x.shape
    fox_w = wqkv.shape[1] // 3
    gw = wug.shape[1] // 2
    n_groups = ws.shape[0]
    row = lambda w: pl.BlockSpec((tm, w), lambda i: (i, 0))
    out_shape = [jax.ShapeDtypeStruct((m, fox_w), BF16),
                 jax.ShapeDtypeStruct((m, fox_w), F32),
                 jax.ShapeDtypeStruct((m, fox_w), F32),
                 jax.ShapeDtypeStruct((m, n_heads), F32),
                 jax.ShapeDtypeStruct((m, gw), BF16)]
    out_specs = [row(fox_w), row(fox_w), row(fox_w), row(n_heads), row(gw)]
    if emit_g:
        out_shape.append(jax.ShapeDtypeStruct((m, gw), F32))
        out_specs.append(row(gw))
    kern = functools.partial(_in_proj_kernel, fox_w=fox_w, n_heads=n_heads, gw=gw,
                             n_groups=n_groups, period=period, q_scale=q_scale)
    return pl.pallas_call(
        kern, grid=(m // tm,),
        in_specs=[row(d), _resident(wqkv.shape), _resident(wf.shape), _resident(bf.shape),
                  _resident(wug.shape), _resident(lg.shape), _resident(lb.shape),
                  _resident(ws.shape), _resident(bs.shape)],
        out_specs=out_specs, out_shape=out_shape,
        compiler_params=_params("parallel"), name="in_proj_sgu",
    )(x, wqkv, wf, bf, wug, lg, lb, ws, bs)


def _cumsum_kernel(x_ref, o_ref, *, rows):
    x = x_ref[...]
    n = x.shape[0]
    hi = lax.Precision.HIGHEST
    ii = lax.broadcasted_iota(jnp.int32, (LANES, LANES), 0)
    jj = lax.broadcasted_iota(jnp.int32, (LANES, LANES), 1)
    within = jnp.dot(x, (ii <= jj).astype(F32), precision=hi, preferred_element_type=F32)
    totals = jnp.dot(x, jnp.ones((LANES, LANES), F32), precision=hi, preferred_element_type=F32)
    ri = lax.broadcasted_iota(jnp.int32, (n, n), 0)
    rj = lax.broadcasted_iota(jnp.int32, (n, n), 1)
    seq_i = jnp.zeros((n, n), jnp.int32)
    seq_j = jnp.zeros((n, n), jnp.int32)
    for s in range(rows, n, rows):
        seq_i += (ri >= s).astype(jnp.int32)
        seq_j += (rj >= s).astype(jnp.int32)
    earlier = ((seq_i == seq_j) & (rj < ri)).astype(F32)
    o_ref[...] = within + jnp.dot(earlier, totals, precision=hi, preferred_element_type=F32)


def _cumsum_rows(x, *, per_step):
    n, rows, _ = x.shape
    x2 = x.reshape(n * rows, LANES)
    spec = pl.BlockSpec((per_step * rows, LANES), lambda i: (i, 0))
    out = pl.pallas_call(
        functools.partial(_cumsum_kernel, rows=rows), grid=(n // per_step,),
        in_specs=[spec], out_specs=spec,
        out_shape=jax.ShapeDtypeStruct(x2.shape, F32),
        compiler_params=_params("parallel"), name="logf_cumsum",
    )(x2)
    return out.reshape(n, rows * LANES)


def _fox_prompt_kernel(q_ref, k_ref, v_ref, c_ref, mask_ref, o_ref, ka_ref, vt_ref, qa_ref, sa_ref, sb_ref,
                       pa_ref, pb_ref, aa_ref, ab_ref, m_ref, acc_ref, *, hk, hpb):
    qi = pl.program_id(2)
    tq = q_ref.shape[1]
    seq = k_ref.shape[1]
    dh = k_ref.shape[2] // hpb
    prep = 512
    ones_rows = (lax.broadcasted_iota(jnp.int32, (BF16_ROWS, prep), 0) == 0).astype(BF16)

    @pl.when(qi == 0)
    def _():
        row = lax.broadcasted_iota(jnp.int32, (dh, prep), 0)
        for hh in range(hpb):
            hc = slice(hh * dh, (hh + 1) * dh)
            for r0 in range(0, seq, prep):
                rs = slice(r0, r0 + prep)
                ka_ref[hh, rs, 0:dh] = k_ref[0, rs, hc].astype(BF16)
                vt_ref[hh, 0:dh, rs] = v_ref[0, rs, hc].T.astype(BF16)
                vt_ref[hh, dh:dh + BF16_ROWS, rs] = ones_rows
                c2 = c_ref[0, hh:hh + 1, rs] * LOG2E
                hi = c2.astype(BF16).astype(F32)
                r1 = c2 - hi
                mid = r1.astype(BF16).astype(F32)
                lo = r1 - mid
                aug = jnp.where(row == 0, -hi, jnp.where(row == 1, -mid, jnp.where(row == 2, -lo, 0.0)))
                ka_ref[hh, rs, dh:2 * dh] = aug.T.astype(BF16)

    ones = (lax.broadcasted_iota(jnp.int32, (dh, tq), 0) < 3).astype(BF16)
    for hh in range(hpb):
        qa_ref[hh, 0:dh, :] = q_ref[0, :, hh * dh:(hh + 1) * dh].astype(F32).T.astype(BF16)
        qa_ref[hh, dh:2 * dh, :] = ones

    def qk(hh, blk):
        start = pl.multiple_of(blk * hk, hk)
        return _dot(ka_ref[hh, pl.ds(start, hk), :], qa_ref[hh])

    def softmax(hh, s, p_ref, a_ref):
        m = m_ref[hh]
        m_new = jnp.maximum(m, jnp.max(s, axis=0, keepdims=True))
        p = jnp.exp2(s - m_new)
        a = jnp.exp2(m - m_new)
        m_ref[hh] = m_new
        p_ref[hh] = p.astype(BF16)
        a_ref[hh] = a

    def pv(hh, blk, p_ref, a_ref):
        start = pl.multiple_of(blk * hk, hk)
        acc_ref[hh] = a_ref[hh] * acc_ref[hh] + _dot(vt_ref[hh, :, pl.ds(start, hk)], p_ref[hh])

    for hh in range(hpb):
        m_ref[hh] = jnp.full((1, tq), -jnp.inf, F32)
        acc_ref[hh] = jnp.zeros((dh + BF16_ROWS, tq), F32)
        pb_ref[hh] = jnp.zeros((hk, tq), BF16)
        ab_ref[hh] = jnp.ones((1, tq), F32)
        sa_ref[hh] = qk(hh, 0)

    def body(j, carry):
        for hh in range(hpb):
            sb_ref[hh] = qk(hh, 2 * j + 1)
            pv(hh, jnp.maximum(2 * j - 1, 0), pb_ref, ab_ref)
            softmax(hh, sa_ref[hh], pa_ref, aa_ref)
            sa_ref[hh] = qk(hh, 2 * j + 2)
            pv(hh, 2 * j, pa_ref, aa_ref)
            softmax(hh, sb_ref[hh], pb_ref, ab_ref)
        return carry

    lax.fori_loop(0, qi, body, 0)
    for hh in range(hpb):
        sb_ref[hh] = qk(hh, 2 * qi + 1)
        pv(hh, jnp.maximum(2 * qi - 1, 0), pb_ref, ab_ref)
        softmax(hh, sa_ref[hh] + mask_ref[0], pa_ref, aa_ref)
        pv(hh, 2 * qi, pa_ref, aa_ref)
        softmax(hh, sb_ref[hh] + mask_ref[1], pb_ref, ab_ref)
        pv(hh, 2 * qi + 1, pb_ref, ab_ref)
        out = acc_ref[hh, 0:dh, :] / acc_ref[hh, dh:dh + 1, :]
        o_ref[0, :, hh * dh:(hh + 1) * dh] = out.T.astype(o_ref.dtype)


def _fox_prompt(q, k, v, c, *, n_heads, hpb):
    b, seq, width = q.shape
    dh = width // n_heads
    hk = 2 * LANES
    tq = 2 * hk
    groups = n_heads // hpb
    cg = c.reshape(b * groups, hpb, seq)
    wide = hpb * dh
    key = jnp.arange(2 * hk, dtype=jnp.int32).reshape(2, hk, 1)
    mask = jnp.where(key <= jnp.arange(tq, dtype=jnp.int32)[None, None, :], 0.0, -jnp.inf).astype(F32)
    return pl.pallas_call(
        functools.partial(_fox_prompt_kernel, hk=hk, hpb=hpb),
        grid=(b, groups, seq // tq),
        in_specs=[pl.BlockSpec((1, tq, wide), lambda bi, h, i: (bi, i, h)),
                  pl.BlockSpec((1, seq, wide), lambda bi, h, i: (bi, 0, h)),
                  pl.BlockSpec((1, seq, wide), lambda bi, h, i: (bi, 0, h)),
                  pl.BlockSpec((1, hpb, seq), lambda bi, h, i: (bi * groups + h, 0, 0)),
                  _resident(mask.shape)],
        out_specs=pl.BlockSpec((1, tq, wide), lambda bi, h, i: (bi, i, h)),
        out_shape=jax.ShapeDtypeStruct((b, seq, width), BF16),
        scratch_shapes=[pltpu.VMEM((hpb, seq, 2 * dh), BF16), pltpu.VMEM((hpb, dh + BF16_ROWS, seq), BF16),
                        pltpu.VMEM((hpb, 2 * dh, tq), BF16),
                        pltpu.VMEM((hpb, hk, tq), F32), pltpu.VMEM((hpb, hk, tq), F32),
                        pltpu.VMEM((hpb, hk, tq), BF16), pltpu.VMEM((hpb, hk, tq), BF16),
                        pltpu.VMEM((hpb, 1, tq), F32), pltpu.VMEM((hpb, 1, tq), F32),
                        pltpu.VMEM((hpb, 1, tq), F32),
                        pltpu.VMEM((hpb, dh + BF16_ROWS, tq), F32)],
        compiler_params=_params("parallel", "parallel", "arbitrary"), name="fox_prompt",
    )(q, k, v, cg, mask)


def _fox_sample_kernel(q_ref, kn_ref, vn_ref, kc_ref, vc_ref, c_ref, o_ref, m_ref, l_ref, acc_ref, *, n_heads):
    ci = pl.program_id(1)
    dh = kc_ref.shape[1]
    tk = kc_ref.shape[0] // n_heads
    t_new = q_ref.shape[1]
    past = pl.num_programs(1) * tk

    def update(scores, values):
        s = jnp.concatenate(scores, axis=0)
        m = m_ref[...]
        m_new = jnp.maximum(m, jnp.max(s, axis=-1, keepdims=True))
        p = jnp.exp2(s - m_new)
        a = jnp.exp2(m - m_new)
        m_ref[...] = m_new
        l_ref[...] = a * l_ref[...] + jnp.sum(p, axis=-1, keepdims=True)
        pb = p.astype(BF16)
        pv = [_dot(pb[h * t_new:(h + 1) * t_new, :], values[h]) for h in range(n_heads)]
        acc_ref[...] = a * acc_ref[...] + jnp.concatenate(pv, axis=0)

    heads = [slice(h * dh, (h + 1) * dh) for h in range(n_heads)]

    @pl.when(ci == 0)
    def _():
        m_ref[...] = jnp.full(m_ref.shape, -jnp.inf, F32)
        l_ref[...] = jnp.zeros(l_ref.shape, F32)
        acc_ref[...] = jnp.zeros(acc_ref.shape, F32)
        r = lax.broadcasted_iota(jnp.int32, (t_new, t_new), 0)
        c = lax.broadcasted_iota(jnp.int32, (t_new, t_new), 1)
        scores = []
        for h, hc in enumerate(heads):
            s = _dot_nt(q_ref[0, :, hc], kn_ref[0, :, hc].astype(BF16))
            s = s - c_ref[0, h:h + 1, past:past + t_new] * LOG2E
            scores.append(jnp.where(c <= r, s, -jnp.inf))
        update(scores, [vn_ref[0, :, hc].astype(BF16) for hc in heads])

    start = pl.multiple_of(ci * tk, tk)
    scores = []
    for h, hc in enumerate(heads):
        s = _dot_nt(q_ref[0, :, hc], kc_ref[pl.ds(h, tk, stride=n_heads), :].astype(BF16))
        scores.append(s - c_ref[0, h:h + 1, pl.ds(start, tk)] * LOG2E)
    update(scores, [vc_ref[pl.ds(h, tk, stride=n_heads), :].astype(BF16) for h in range(n_heads)])

    @pl.when(ci == pl.num_programs(1) - 1)
    def _():
        out = acc_ref[...] / l_ref[...]
        for h, hc in enumerate(heads):
            o_ref[0, :, hc] = out[h * t_new:(h + 1) * t_new, :].astype(o_ref.dtype)


def _fox_sample(q, kn, vn, kc, vc, c, *, tk):
    b, t_new, width = q.shape
    dh = kc.shape[1]
    n_heads = width // dh
    past = kc.shape[0] // (b * n_heads)
    chunks = past // tk
    new = pl.BlockSpec((1, t_new, width), lambda bi, ci: (bi, 0, 0))
    old = pl.BlockSpec((tk * n_heads, dh), lambda bi, ci: (bi * chunks + ci, 0))
    return pl.pallas_call(
        functools.partial(_fox_sample_kernel, n_heads=n_heads), grid=(b, chunks),
        in_specs=[new, new, new, old, old,
                  pl.BlockSpec((1, n_heads, c.shape[2]), lambda bi, ci: (bi, 0, 0))],
        out_specs=new,
        out_shape=jax.ShapeDtypeStruct((b, t_new, width), BF16),
        scratch_shapes=[pltpu.VMEM((n_heads * t_new, 1), F32), pltpu.VMEM((n_heads * t_new, 1), F32),
                        pltpu.VMEM((n_heads * t_new, dh), F32)],
        compiler_params=_params("parallel", "arbitrary"), name="fox_sample",
    )(q, kn, vn, kc, vc, c)


def _out_proj_kernel(fo_ref, go_ref, x_ref, w_ref, g_ref, b_ref, o_ref, *, alpha, splits):
    half = fo_ref.shape[1]
    sub = x_ref.shape[0] // splits
    for r0 in range(0, x_ref.shape[0], sub):
        rows = slice(r0, r0 + sub)
        mix = _dot(fo_ref[rows, :], w_ref[0:half, :]) + _dot(go_ref[rows, :], w_ref[half:2 * half, :])
        o_ref[rows, :] = _layer_norm(alpha * x_ref[rows, :] + mix, g_ref[...], b_ref[...])


def _out_proj(fo, go, x, w, g, b, *, tm, alpha, splits):
    m, d = x.shape
    row = lambda wd: pl.BlockSpec((tm, wd), lambda i: (i, 0))
    return pl.pallas_call(
        functools.partial(_out_proj_kernel, alpha=alpha, splits=splits), grid=(m // tm,),
        in_specs=[row(fo.shape[1]), row(go.shape[1]), row(d),
                  _resident(w.shape), _resident(g.shape), _resident(b.shape)],
        out_specs=row(d), out_shape=jax.ShapeDtypeStruct((m, d), F32),
        compiler_params=_params("parallel"), name="out_proj_ln",
    )(fo, go, x, w, g, b)


def _mem_proj_kernel(x_ref, wk_ref, wv_ref, kt_ref, vt_ref, kb_ref, vb_ref, *, n_heads):
    tm, d = x_ref.shape
    lane_tiles = d // n_heads // LANES
    stride = lane_tiles * n_heads
    xb = x_ref[...].astype(BF16)
    for w_ref, t_ref, b_ref in ((wk_ref, kt_ref, kb_ref), (wv_ref, vt_ref, vb_ref)):
        y = _dot(xb, w_ref[...])
        b_ref[...] = y.astype(BF16)
        for hd in range(n_heads):
            for j in range(lane_tiles):
                c0 = (hd * lane_tiles + j) * LANES
                t_ref[pl.ds(j * n_heads + hd, tm, stride=stride), :] = y[:, c0:c0 + LANES]


def _mem_proj(x, wk, wv, *, tm, n_heads):
    m, d = x.shape
    rows = pl.BlockSpec((tm, d), lambda i: (i, 0))
    tiled = pl.BlockSpec((tm * d // LANES, LANES), lambda i: (i, 0))
    return pl.pallas_call(
        functools.partial(_mem_proj_kernel, n_heads=n_heads), grid=(m // tm,),
        in_specs=[rows, _resident(wk.shape), _resident(wv.shape)],
        out_specs=[tiled, tiled, rows, rows],
        out_shape=[jax.ShapeDtypeStruct((m * d // LANES, LANES), F32)] * 2
        + [jax.ShapeDtypeStruct((m, d), BF16)] * 2,
        compiler_params=_params("parallel"), name="mem_proj",
    )(x, wk, wv)


def _mem_attn_kernel(h_ref, wq_ref, wo_ref, mk_ref, mv_ref, g_ref, b_ref, o_ref, att_ref,
                     *, n_heads, n_mem, nb, alpha, scale, splits):
    tm, d = h_ref.shape
    dh = d // n_heads
    lane_tiles = dh // LANES

    def mem_head(ref, bi, hd):
        if len(ref.shape) == 3:
            return ref[bi, :, hd * dh:(hd + 1) * dh].astype(BF16)
        stride = lane_tiles * n_heads
        parts = [ref[pl.ds(bi * n_mem * stride + j * n_heads + hd, n_mem, stride=stride), :]
                 for j in range(lane_tiles)]
        return jnp.concatenate(parts, axis=1).astype(BF16)

    groups = splits if nb == 1 else 1
    rg = tm // groups
    rb = rg // nb
    for gi in range(groups):
        rows = slice(gi * rg, (gi + 1) * rg)
        h = h_ref[rows, :]
        qb = (_dot(h.astype(BF16), wq_ref[...]) * scale).astype(BF16)
        for bi in range(nb):
            for hd in range(n_heads):
                cols = slice(hd * dh, (hd + 1) * dh)
                s = _dot_nt(qb[bi * rb:(bi + 1) * rb, cols], mem_head(mk_ref, bi, hd))
                p = jnp.exp(s - jnp.max(s, axis=-1, keepdims=True))
                l = jnp.sum(p, axis=-1, keepdims=True)
                o = _dot(p.astype(BF16), mem_head(mv_ref, bi, hd))
                att_ref[gi * rg + bi * rb:gi * rg + (bi + 1) * rb, cols] = (o / l).astype(BF16)
        y = _dot(att_ref[rows, :], wo_ref[...])
        o_ref[rows, :] = _layer_norm(alpha * h + y, g_ref[...], b_ref[...])


def _mem_attn(h, wq, wo, mk, mv, g, b, *, tm, rows_per_batch, n_heads, n_mem, alpha, scale, splits):
    m, d = h.shape
    if tm <= rows_per_batch:
        nb = 1
        steps_per_batch = rows_per_batch // tm
        batch_of = lambda i: i // steps_per_batch
    else:
        nb = tm // rows_per_batch
        batch_of = lambda i: i
    row = pl.BlockSpec((tm, d), lambda i: (i, 0))
    if mk.ndim == 3:
        mem = pl.BlockSpec((nb, n_mem, d), lambda i: (batch_of(i), 0, 0))
    else:
        mem = pl.BlockSpec((nb * n_mem * d // LANES, LANES), lambda i: (batch_of(i), 0))
    return pl.pallas_call(
        functools.partial(_mem_attn_kernel, n_heads=n_heads, n_mem=n_mem, nb=nb, alpha=alpha, scale=scale,
                          splits=splits),
        grid=(m // tm,),
        in_specs=[row, _resident(wq.shape), _resident(wo.shape), mem, mem,
                  _resident(g.shape), _resident(b.shape)],
        out_specs=row, out_shape=jax.ShapeDtypeStruct((m, d), F32),
        scratch_shapes=[pltpu.VMEM((tm, d), BF16)],
        compiler_params=_params("parallel"), name="mem_attn_ln",
    )(h, wq, wo, mk, mv, g, b)


def _ffn_kernel(x_ref, wu_ref, wd_ref, g_ref, b_ref, o_ref, xb_ref, *, alpha):
    f = pl.program_id(1)

    @pl.when(f == 0)
    def _():
        xb_ref[...] = x_ref[...].astype(BF16)
        o_ref[...] = jnp.zeros_like(o_ref)

    a = jnp.maximum(_dot(xb_ref[...], wu_ref[...]), 0.0)
    ab = (a * a).astype(BF16)
    slab = 512
    for n0 in range(0, o_ref.shape[1], slab):
        o_ref[:, n0:n0 + slab] += _dot(ab, wd_ref[:, n0:n0 + slab])

    @pl.when(f == pl.num_programs(1) - 1)
    def _():
        o_ref[...] = _layer_norm(alpha * x_ref[...] + o_ref[...], g_ref[...], b_ref[...])


def _ffn(x, wu, wd, g, b, *, tm, tf, alpha):
    m, d = x.shape
    dff = wu.shape[1]
    row = pl.BlockSpec((tm, d), lambda i, f: (i, 0))
    return pl.pallas_call(
        functools.partial(_ffn_kernel, alpha=alpha), grid=(m // tm, dff // tf),
        in_specs=[row, pl.BlockSpec((d, tf), lambda i, f: (0, f)),
                  pl.BlockSpec((tf, d), lambda i, f: (f, 0)),
                  _resident(g.shape), _resident(b.shape)],
        out_specs=row, out_shape=jax.ShapeDtypeStruct((m, d), F32),
        scratch_shapes=[pltpu.VMEM((tm, d), BF16)],
        compiler_params=_params("parallel", "arbitrary"), name="ffn_ln",
    )(x, wu, wd, g, b)


def _pad_lanes(a):
    return jnp.pad(a, ((0, 0), (0, LANES - a.shape[1])))


def kernel(x_prompt, x_sample, mem_prompt, cache_fox_k, cache_fox_v, cache_fox_logf, cache_mem_k, cache_mem_v, w_in, b_f, sgu_ln_g, sgu_ln_b, w_s, b_s, w_out, ln1_g, ln1_b, w_mq, w_mk, w_mv, w_mo, ln2_g, ln2_b, w_up, w_down, ln3_g, ln3_b):
    depth = w_in.shape[0]
    assert depth == 1
    b, seq, d = x_prompt.shape
    bs_, t_new, _ = x_sample.shape
    past, n_heads, dh = cache_fox_k.shape[2:]
    n_mem, mem_heads, mem_dh = cache_mem_k.shape[2:]
    n_groups, chunk = w_s.shape[1], w_s.shape[2]
    fox_w = n_heads * dh
    gw = d - fox_w
    alpha = (2 * depth) ** 0.25
    l = 0

    wi = w_in[l]
    o3 = 3 * fox_w
    o4 = o3 + n_heads
    wqkv = wi[:, :o3].astype(BF16)
    wf = _pad_lanes(wi[:, o3:o4]).astype(BF16)
    bf = _pad_lanes(b_f[l][None, :])
    wug = wi[:, o4:].astype(BF16)
    lg, lb = sgu_ln_g[l][None, :], sgu_ln_b[l][None, :]
    ws_p, bs_p = w_s[l], b_s[l][:, :, None]
    reps = chunk // t_new
    ws_s = jnp.tile(w_s[l][:, :t_new, :t_new], (1, reps, reps))
    bs_s = jnp.tile(b_s[l][:, :t_new], (1, reps))[:, :, None]
    wo = w_out[l].astype(BF16)
    wmq, wmk, wmv, wmo = (w[l].astype(BF16) for w in (w_mq, w_mk, w_mv, w_mo))
    wu, wd = w_up[l].astype(BF16), w_down[l].astype(BF16)
    ln = [a[l][None, :] for a in (ln1_g, ln1_b, ln2_g, ln2_b, ln3_g, ln3_b)]

    def post(h, fo, go, mk, mv, *, rows_per_batch, tm_attn):
        h = _out_proj(fo, go, h, wo, ln[0], ln[1], tm=512, alpha=alpha, splits=2)
        h = _mem_attn(h, wmq, wmo, mk, mv, ln[2], ln[3], tm=tm_attn, rows_per_batch=rows_per_batch,
                      n_heads=mem_heads, n_mem=n_mem, alpha=alpha, scale=mem_dh ** -0.5, splits=2)
        return _ffn(h, wu, wd, ln[4], ln[5], tm=512, tf=1024, alpha=alpha)

    xp = x_prompt.reshape(b * seq, d)
    q, k, v, logf, go = _in_proj(xp, wqkv, wf, bf, wug, lg, lb, ws_p, bs_p, tm=512, period=chunk,
                                 n_heads=n_heads, q_scale=dh ** -0.5 * LOG2E, emit_g=False)
    lt = jnp.transpose(logf.reshape(b, seq, n_heads), (0, 2, 1)).reshape(b * n_heads, seq // LANES, LANES)
    c = _cumsum_rows(lt, per_step=n_heads)
    fo = _fox_prompt(q.reshape(b, seq, fox_w), k.reshape(b, seq, fox_w), v.reshape(b, seq, fox_w), c,
                     n_heads=n_heads, hpb=4)
    mem2 = mem_prompt.reshape(b * n_mem, d)
    mkt, mvt, mkb, mvb = _mem_proj(mem2, wmk, wmv, tm=512, n_heads=mem_heads)

    def untile_rows(a):
        a = a.reshape(b, n_mem, mem_dh // LANES, mem_heads, LANES)
        return jnp.transpose(a, (0, 1, 3, 2, 4)).reshape(1, b, n_mem, mem_heads, mem_dh)

    mk, mv = untile_rows(mkt), untile_rows(mvt)
    yp = post(xp, fo.reshape(b * seq, fox_w), go, mkb.reshape(b, n_mem, d), mvb.reshape(b, n_mem, d),
              rows_per_batch=seq, tm_attn=512)

    xs = x_sample.reshape(bs_ * t_new, d)
    qs, ks, vs, logfs, gos, gs = _in_proj(xs, wqkv, wf, bf, wug, lg, lb, ws_s, bs_s, tm=bs_ * t_new,
                                          period=t_new, n_heads=n_heads, q_scale=dh ** -0.5 * LOG2E,
                                          emit_g=True)
    rows_c = -(-(past + t_new) // LANES)
    rows_c = -(-rows_c // 8) * 8
    lcat = jnp.concatenate([cache_fox_logf[l], logfs.reshape(bs_, t_new, n_heads)], axis=1)
    lcat = jnp.pad(lcat, ((0, 0), (0, rows_c * LANES - past - t_new), (0, 0)))
    lts = jnp.transpose(lcat, (0, 2, 1)).reshape(bs_ * n_heads, rows_c, LANES)
    cs = _cumsum_rows(lts, per_step=n_heads).reshape(bs_, n_heads, rows_c * LANES)
    fos = _fox_sample(qs.reshape(bs_, t_new, fox_w), ks.reshape(bs_, t_new, fox_w), vs.reshape(bs_, t_new, fox_w),
                      cache_fox_k.reshape(bs_ * past * n_heads, dh), cache_fox_v.reshape(bs_ * past * n_heads, dh),
                      cs, tk=1024)
    def tile_rows(a):
        a = a.reshape(bs_, n_mem, mem_heads, mem_dh // LANES, LANES)
        return jnp.transpose(a, (0, 1, 3, 2, 4)).reshape(bs_ * n_mem * d // LANES, LANES)

    ys = post(xs, fos.reshape(bs_ * t_new, fox_w), gos, tile_rows(cache_mem_k), tile_rows(cache_mem_v),
              rows_per_batch=t_new, tm_attn=64)

    return (yp.reshape(b, seq, d), ys.reshape(bs_, t_new, d),
            k.reshape(1, b, seq, n_heads, dh), v.reshape(1, b, seq, n_heads, dh),
            logf.reshape(1, b, seq, n_heads),
            mk, mv,
            ks.reshape(1, bs_, t_new, n_heads, dh), vs.reshape(1, bs_, t_new, n_heads, dh),
            logfs.reshape(1, bs_, t_new, n_heads), gs.reshape(1, bs_, t_new, gw))
```

```python
import functools
import math

import jax
import jax.numpy as jnp
from jax import lax
from jax.experimental import pallas as pl
from jax.experimental.pallas import tpu as pltpu

F32 = jnp.float32
BF16 = jnp.bfloat16

LN_EPS = 1e-5
LANES = 128
BF16_ROWS = 16
GELU_C = math.sqrt(2.0 / math.pi)
LOG2E = 1.0 / math.log(2.0)
VMEM_LIMIT = 56 * 1024 * 1024
VMEM_LIMIT_FOX = 60 * 1024 * 1024


def _dot(a, b):
    return jnp.dot(a, b, preferred_element_type=F32)


def _dot_nt(a, b):
    return lax.dot_general(a, b, (((1,), (1,)), ((), ())), preferred_element_type=F32)


def _gelu(x):
    return 0.5 * x * (1.0 + jnp.tanh(GELU_C * (x + 0.044715 * (x * x * x))))


def _layer_norm(x, g, b):
    mu = jnp.mean(x, axis=-1, keepdims=True)
    xc = x - mu
    var = jnp.mean(xc * xc, axis=-1, keepdims=True)
    return xc * lax.rsqrt(var + LN_EPS) * g + b


def _log_sigmoid(x):
    return jnp.minimum(x, 0.0) - jnp.log1p(jnp.exp(-jnp.abs(x)))


def _resident(shape):
    nd = len(shape)
    return pl.BlockSpec(shape, lambda *_: (0,) * nd, pipeline_mode=pl.Buffered(1))


def _params(*sem, vmem=VMEM_LIMIT):
    return pltpu.CompilerParams(dimension_semantics=sem, vmem_limit_bytes=vmem)


def _in_proj_kernel(x_ref, wqkv_ref, wf_ref, bf_ref, wug_ref, lg_ref, lb_ref, ws_ref, bs_ref,
                    q_ref, k_ref, v_ref, logf_ref, go_ref, *g_ref,
                    fox_w, n_heads, gw, n_groups, period, q_scale):
    tm = x_ref.shape[0]
    chunk = ws_ref.shape[1]
    gd = gw // n_groups
    xb = x_ref[...].astype(BF16)
    q_ref[...] = (_dot(xb, wqkv_ref[:, 0:fox_w]) * q_scale).astype(BF16)
    k_ref[...] = _dot(xb, wqkv_ref[:, fox_w:2 * fox_w])
    v_ref[...] = _dot(xb, wqkv_ref[:, 2 * fox_w:3 * fox_w])
    zf = _dot(xb, wf_ref[...]) + bf_ref[...]
    logf_ref[...] = _log_sigmoid(zf)[:, :n_heads]
    g = _layer_norm(_gelu(_dot(xb, wug_ref[:, gw:2 * gw])), lg_ref[...], lb_ref[...])
    if g_ref:
        g_ref[0][...] = g
    gb = g.astype(BF16)
    u = _gelu(_dot(xb, wug_ref[:, 0:gw]))
    r = lax.broadcasted_iota(jnp.int32, (chunk, chunk), 0)
    c = lax.broadcasted_iota(jnp.int32, (chunk, chunk), 1)
    sh = period.bit_length() - 1
    keep = ((r >> sh) == (c >> sh)) & ((c & (period - 1)) <= (r & (period - 1)))
    for gi in range(n_groups):
        wsg = jnp.where(keep, ws_ref[gi], 0.0).astype(BF16)
        cols = slice(gi * gd, (gi + 1) * gd)
        for ci in range(tm // chunk):
            rows = slice(ci * chunk, (ci + 1) * chunk)
            s = _dot(wsg, gb[rows, cols]) + bs_ref[gi]
            go_ref[rows, cols] = (u[rows, cols] * s).astype(BF16)


def _in_proj(x, wqkv, wf, bf, wug, lg, lb, ws, bs, *, tm, period, n_heads, q_scale, emit_g):
    m, d = x.shape
    fox_w = wqkv.shape[1] // 3
    gw = wug.shape[1] // 2
    n_groups = ws.shape[0]
    row = lambda w: pl.BlockSpec((tm, w), lambda i: (i, 0))
    out_shape = [jax.ShapeDtypeStruct((m, fox_w), BF16),
                 jax.ShapeDtypeStruct((m, fox_w), F32),
                 jax.ShapeDtypeStruct((m, fox_w), F32),
                 jax.ShapeDtypeStruct((m, n_heads), F32),
                 jax.ShapeDtypeStruct((m, gw), BF16)]
    out_specs = [row(fox_w), row(fox_w), row(fox_w), row(n_heads), row(gw)]
    if emit_g:
        out_shape.append(jax.ShapeDtypeStruct((m, gw), F32))
        out_specs.append(row(gw))
    kern = functools.partial(_in_proj_kernel, fox_w=fox_w, n_heads=n_heads, gw=gw,
                             n_groups=n_groups, period=period, q_scale=q_scale)
    return pl.pallas_call(
        kern, grid=(m // tm,),
        in_specs=[row(d), _resident(wqkv.shape), _resident(wf.shape), _resident(bf.shape),
                  _resident(wug.shape), _resident(lg.shape), _resident(lb.shape),
                  _resident(ws.shape), _resident(bs.shape)],
        out_specs=out_specs, out_shape=out_shape,
        compiler_params=_params("parallel"), name="in_proj_sgu",
    )(x, wqkv, wf, bf, wug, lg, lb, ws, bs)


def _cumsum_kernel(x_ref, o_ref, *, rows):
    x = x_ref[...]
    n = x.shape[0]
    hi = lax.Precision.HIGHEST
    ii = lax.broadcasted_iota(jnp.int32, (LANES, LANES), 0)
    jj = lax.broadcasted_iota(jnp.int32, (LANES, LANES), 1)
    within = jnp.dot(x, (ii <= jj).astype(F32), precision=hi, preferred_element_type=F32)
    totals = jnp.dot(x, jnp.ones((LANES, LANES), F32), precision=hi, preferred_element_type=F32)
    ri = lax.broadcasted_iota(jnp.int32, (n, n), 0)
    rj = lax.broadcasted_iota(jnp.int32, (n, n), 1)
    seq_i = jnp.zeros((n, n), jnp.int32)
    seq_j = jnp.zeros((n, n), jnp.int32)
    for s in range(rows, n, rows):
        seq_i += (ri >= s).astype(jnp.int32)
        seq_j += (rj >= s).astype(jnp.int32)
    earlier = ((seq_i == seq_j) & (rj < ri)).astype(F32)
    o_ref[...] = within + jnp.dot(earlier, totals, precision=hi, preferred_element_type=F32)


def _cumsum_rows(x, *, per_step):
    n, rows, _ = x.shape
    x2 = x.reshape(n * rows, LANES)
    spec = pl.BlockSpec((per_step * rows, LANES), lambda i: (i, 0))
    out = pl.pallas_call(
        functools.partial(_cumsum_kernel, rows=rows), grid=(n // per_step,),
        in_specs=[spec], out_specs=spec,
        out_shape=jax.ShapeDtypeStruct(x2.shape, F32),
        compiler_params=_params("parallel"), name="logf_cumsum",
    )(x2)
    return out.reshape(n, rows * LANES)


def _fox_prompt_kernel(q_ref, k_ref, v_ref, c_ref, mask_ref, o_ref, ka_ref, vt_ref, qa_ref, sa_ref, sb_ref,
                       pa_ref, pb_ref, aa_ref, ab_ref, m_ref, acc_ref, *, hk, hpb):
    qi = pl.program_id(2)
    tq = q_ref.shape[1]
    seq = k_ref.shape[1]
    dh = k_ref.shape[2] // hpb
    prep = 512
    ones_rows = (lax.broadcasted_iota(jnp.int32, (BF16_ROWS, prep), 0) == 0).astype(BF16)

    @pl.when(qi == 0)
    def _():
        row = lax.broadcasted_iota(jnp.int32, (dh, prep), 0)
        for hh in range(hpb):
            hc = slice(hh * dh, (hh + 1) * dh)
            for r0 in range(0, seq, prep):
                rs = slice(r0, r0 + prep)
                ka_ref[hh, rs, 0:dh] = k_ref[0, rs, hc].astype(BF16)
                vt_ref[hh, 0:dh, rs] = v_ref[0, rs, hc].T.astype(BF16)
                vt_ref[hh, dh:dh + BF16_ROWS, rs] = ones_rows
                c2 = c_ref[0, hh:hh + 1, rs] * LOG2E
                hi = c2.astype(BF16).astype(F32)
                r1 = c2 - hi
                mid = r1.astype(BF16).astype(F32)
                lo = r1 - mid
                aug = jnp.where(row == 0, -hi, jnp.where(row == 1, -mid, jnp.where(row == 2, -lo, 0.0)))
                ka_ref[hh, rs, dh:2 * dh] = aug.T.astype(BF16)

    ones = (lax.broadcasted_iota(jnp.int32, (dh, tq), 0) < 3).astype(BF16)
    for hh in range(hpb):
        qa_ref[hh, 0:dh, :] = q_ref[0, :, hh * dh:(hh + 1) * dh].astype(F32).T.astype(BF16)
        qa_ref[hh, dh:2 * dh, :] = ones

    def qk(hh, blk):
        start = pl.multiple_of(blk * hk, hk)
        return _dot(ka_ref[hh, pl.ds(start, hk), :], qa_ref[hh])

    def softmax(hh, s, p_ref, a_ref):
        m = m_ref[hh]
        m_new = jnp.maximum(m, jnp.max(s, axis=0, keepdims=True))
        p = jnp.exp2(s - m_new)
        a = jnp.exp2(m - m_new)
        m_ref[hh] = m_new
        p_ref[hh] = p.astype(BF16)
        a_ref[hh] = a

    def pv(hh, blk, p_ref, a_ref):
        start = pl.multiple_of(blk * hk, hk)
        acc_ref[hh] = a_ref[hh] * acc_ref[hh] + _dot(vt_ref[hh, :, pl.ds(start, hk)], p_ref[hh])

    for hh in range(hpb):
        m_ref[hh] = jnp.full((1, tq), -jnp.inf, F32)
        acc_ref[hh] = jnp.zeros((dh + BF16_ROWS, tq), F32)
        pb_ref[hh] = jnp.zeros((hk, tq), BF16)
        ab_ref[hh] = jnp.ones((1, tq), F32)
        sa_ref[hh] = qk(hh, 0)

    def body(j, carry):
        for hh in range(hpb):
            sb_ref[hh] = qk(hh, 2 * j + 1)
            pv(hh, jnp.maximum(2 * j - 1, 0), pb_ref, ab_ref)
            softmax(hh, sa_ref[hh], pa_ref, aa_ref)
            sa_ref[hh] = qk(hh, 2 * j + 2)
            pv(hh, 2 * j, pa_ref, aa_ref)
            softmax(hh, sb_ref[hh], pb_ref, ab_ref)
        return carry

    lax.fori_loop(0, qi, body, 0)
    for hh in range(hpb):
        sb_ref[hh] = qk(hh, 2 * qi + 1)
        pv(hh, jnp.maximum(2 * qi - 1, 0), pb_ref, ab_ref)
        softmax(hh, sa_ref[hh] + mask_ref[0], pa_ref, aa_ref)
        pv(hh, 2 * qi, pa_ref, aa_ref)
        softmax(hh, sb_ref[hh] + mask_ref[1], pb_ref, ab_ref)
        pv(hh, 2 * qi + 1, pb_ref, ab_ref)
        out = acc_ref[hh, 0:dh, :] / acc_ref[hh, dh:dh + 1, :]
        o_ref[0, :, hh * dh:(hh + 1) * dh] = out.T.astype(o_ref.dtype)


def _fox_prompt(q, k, v, c, *, n_heads, hpb):
    b, seq, width = q.shape
    dh = width // n_heads
    hk = 2 * LANES
    tq = 2 * hk
    groups = n_heads // hpb
    cg = c.reshape(b * groups, hpb, seq)
    wide = hpb * dh
    key = jnp.arange(2 * hk, dtype=jnp.int32).reshape(2, hk, 1)
    mask = jnp.where(key <= jnp.arange(tq, dtype=jnp.int32)[None, None, :], 0.0, -jnp.inf).astype(F32)
    return pl.pallas_call(
        functools.partial(_fox_prompt_kernel, hk=hk, hpb=hpb),
        grid=(b, groups, seq // tq),
        in_specs=[pl.BlockSpec((1, tq, wide), lambda bi, h, i: (bi, i, h)),
                  pl.BlockSpec((1, seq, wide), lambda bi, h, i: (bi, 0, h)),
                  pl.BlockSpec((1, seq, wide), lambda bi, h, i: (bi, 0, h)),
                  pl.BlockSpec((1, hpb, seq), lambda bi, h, i: (bi * groups + h, 0, 0)),
                  _resident(mask.shape)],
        out_specs=pl.BlockSpec((1, tq, wide), lambda bi, h, i: (bi, i, h)),
        out_shape=jax.ShapeDtypeStruct((b, seq, width), BF16),
        scratch_shapes=[pltpu.VMEM((hpb, seq, 2 * dh), BF16), pltpu.VMEM((hpb, dh + BF16_ROWS, seq), BF16),
                        pltpu.VMEM((hpb, 2 * dh, tq), BF16),
                        pltpu.VMEM((hpb, hk, tq), F32), pltpu.VMEM((hpb, hk, tq), F32),
                        pltpu.VMEM((hpb, hk, tq), BF16), pltpu.VMEM((hpb, hk, tq), BF16),
                        pltpu.VMEM((hpb, 1, tq), F32), pltpu.VMEM((hpb, 1, tq), F32),
                        pltpu.VMEM((hpb, 1, tq), F32),
                        pltpu.VMEM((hpb, dh + BF16_ROWS, tq), F32)],
        compiler_params=_params("parallel", "parallel", "arbitrary", vmem=VMEM_LIMIT_FOX), name="fox_prompt",
    )(q, k, v, cg, mask)


def _fox_sample_kernel(q_ref, kn_ref, vn_ref, kc_ref, vc_ref, c_ref, o_ref, m_ref, l_ref, acc_ref, *, n_heads):
    ci = pl.program_id(1)
    dh = kc_ref.shape[1]
    tk = kc_ref.shape[0] // n_heads
    t_new = q_ref.shape[1]
    past = pl.num_programs(1) * tk

    def update(scores, values):
        s = jnp.concatenate(scores, axis=0)
        m = m_ref[...]
        m_new = jnp.maximum(m, jnp.max(s, axis=-1, keepdims=True))
        p = jnp.exp2(s - m_new)
        a = jnp.exp2(m - m_new)
        m_ref[...] = m_new
        l_ref[...] = a * l_ref[...] + jnp.sum(p, axis=-1, keepdims=True)
        pb = p.astype(BF16)
        pv = [_dot(pb[h * t_new:(h + 1) * t_new, :], values[h]) for h in range(n_heads)]
        acc_ref[...] = a * acc_ref[...] + jnp.concatenate(pv, axis=0)

    heads = [slice(h * dh, (h + 1) * dh) for h in range(n_heads)]

    @pl.when(ci == 0)
    def _():
        m_ref[...] = jnp.full(m_ref.shape, -jnp.inf, F32)
        l_ref[...] = jnp.zeros(l_ref.shape, F32)
        acc_ref[...] = jnp.zeros(acc_ref.shape, F32)
        r = lax.broadcasted_iota(jnp.int32, (t_new, t_new), 0)
        c = lax.broadcasted_iota(jnp.int32, (t_new, t_new), 1)
        scores = []
        for h, hc in enumerate(heads):
            s = _dot_nt(q_ref[0, :, hc], kn_ref[0, :, hc].astype(BF16))
            s = s - c_ref[0, h:h + 1, past:past + t_new] * LOG2E
            scores.append(jnp.where(c <= r, s, -jnp.inf))
        update(scores, [vn_ref[0, :, hc].astype(BF16) for hc in heads])

    start = pl.multiple_of(ci * tk, tk)
    scores = []
    for h, hc in enumerate(heads):
        s = _dot_nt(q_ref[0, :, hc], kc_ref[pl.ds(h, tk, stride=n_heads), :].astype(BF16))
        scores.append(s - c_ref[0, h:h + 1, pl.ds(start, tk)] * LOG2E)
    update(scores, [vc_ref[pl.ds(h, tk, stride=n_heads), :].astype(BF16) for h in range(n_heads)])

    @pl.when(ci == pl.num_programs(1) - 1)
    def _():
        out = acc_ref[...] / l_ref[...]
        for h, hc in enumerate(heads):
            o_ref[0, :, hc] = out[h * t_new:(h + 1) * t_new, :].astype(o_ref.dtype)


def _fox_sample(q, kn, vn, kc, vc, c, *, tk):
    b, t_new, width = q.shape
    dh = kc.shape[1]
    n_heads = width // dh
    past = kc.shape[0] // (b * n_heads)
    chunks = past // tk
    new = pl.BlockSpec((1, t_new, width), lambda bi, ci: (bi, 0, 0))
    old = pl.BlockSpec((tk * n_heads, dh), lambda bi, ci: (bi * chunks + ci, 0))
    return pl.pallas_call(
        functools.partial(_fox_sample_kernel, n_heads=n_heads), grid=(b, chunks),
        in_specs=[new, new, new, old, old,
                  pl.BlockSpec((1, n_heads, c.shape[2]), lambda bi, ci: (bi, 0, 0))],
        out_specs=new,
        out_shape=jax.ShapeDtypeStruct((b, t_new, width), BF16),
        scratch_shapes=[pltpu.VMEM((n_heads * t_new, 1), F32), pltpu.VMEM((n_heads * t_new, 1), F32),
                        pltpu.VMEM((n_heads * t_new, dh), F32)],
        compiler_params=_params("parallel", "arbitrary"), name="fox_sample",
    )(q, kn, vn, kc, vc, c)


def _out_proj_kernel(fo_ref, go_ref, x_ref, w_ref, g_ref, b_ref, o_ref, *, alpha, splits):
    half = fo_ref.shape[1]
    sub = x_ref.shape[0] // splits
    for r0 in range(0, x_ref.shape[0], sub):
        rows = slice(r0, r0 + sub)
        mix = _dot(fo_ref[rows, :], w_ref[0:half, :]) + _dot(go_ref[rows, :], w_ref[half:2 * half, :])
        o_ref[rows, :] = _layer_norm(alpha * x_ref[rows, :] + mix, g_ref[...], b_ref[...])


def _out_proj(fo, go, x, w, g, b, *, tm, alpha, splits):
    m, d = x.shape
    row = lambda wd: pl.BlockSpec((tm, wd), lambda i: (i, 0))
    return pl.pallas_call(
        functools.partial(_out_proj_kernel, alpha=alpha, splits=splits), grid=(m // tm,),
        in_specs=[row(fo.shape[1]), row(go.shape[1]), row(d),
                  _resident(w.shape), _resident(g.shape), _resident(b.shape)],
        out_specs=row(d), out_shape=jax.ShapeDtypeStruct((m, d), F32),
        compiler_params=_params("parallel"), name="out_proj_ln",
    )(fo, go, x, w, g, b)


def _mem_proj_kernel(x_ref, wk_ref, wv_ref, kt_ref, vt_ref, kb_ref, vb_ref, *, n_heads):
    tm, d = x_ref.shape
    lane_tiles = d // n_heads // LANES
    stride = lane_tiles * n_heads
    xb = x_ref[...].astype(BF16)
    for w_ref, t_ref, b_ref in ((wk_ref, kt_ref, kb_ref), (wv_ref, vt_ref, vb_ref)):
        y = _dot(xb, w_ref[...])
        b_ref[...] = y.astype(BF16)
        for hd in range(n_heads):
            for j in range(lane_tiles):
                c0 = (hd * lane_tiles + j) * LANES
                t_ref[pl.ds(j * n_heads + hd, tm, stride=stride), :] = y[:, c0:c0 + LANES]


def _mem_proj(x, wk, wv, *, tm, n_heads):
    m, d = x.shape
    rows = pl.BlockSpec((tm, d), lambda i: (i, 0))
    tiled = pl.BlockSpec((tm * d // LANES, LANES), lambda i: (i, 0))
    return pl.pallas_call(
        functools.partial(_mem_proj_kernel, n_heads=n_heads), grid=(m // tm,),
        in_specs=[rows, _resident(wk.shape), _resident(wv.shape)],
        out_specs=[tiled, tiled, rows, rows],
        out_shape=[jax.ShapeDtypeStruct((m * d // LANES, LANES), F32)] * 2
        + [jax.ShapeDtypeStruct((m, d), BF16)] * 2,
        compiler_params=_params("parallel"), name="mem_proj",
    )(x, wk, wv)


def _mem_attn_kernel(h_ref, wq_ref, wo_ref, mk_ref, mv_ref, g_ref, b_ref, o_ref, att_ref,
                     *, n_heads, n_mem, nb, alpha, scale, splits):
    tm, d = h_ref.shape
    dh = d // n_heads
    lane_tiles = dh // LANES

    def mem_head(ref, bi, hd):
        if len(ref.shape) == 3:
            return ref[bi, :, hd * dh:(hd + 1) * dh].astype(BF16)
        stride = lane_tiles * n_heads
        parts = [ref[pl.ds(bi * n_mem * stride + j * n_heads + hd, n_mem, stride=stride), :]
                 for j in range(lane_tiles)]
        return jnp.concatenate(parts, axis=1).astype(BF16)

    groups = splits if nb == 1 else 1
    rg = tm // groups
    rb = rg // nb
    for gi in range(groups):
        rows = slice(gi * rg, (gi + 1) * rg)
        h = h_ref[rows, :]
        qb = (_dot(h.astype(BF16), wq_ref[...]) * scale).astype(BF16)
        pairs = [(bi, hd) for bi in range(nb) for hd in range(n_heads)]
        s = jnp.concatenate([_dot_nt(qb[bi * rb:(bi + 1) * rb, hd * dh:(hd + 1) * dh], mem_head(mk_ref, bi, hd))
                             for bi, hd in pairs], axis=0)
        p = jnp.exp(s - jnp.max(s, axis=-1, keepdims=True))
        l = jnp.sum(p, axis=-1, keepdims=True)
        pb = p.astype(BF16)
        for k, (bi, hd) in enumerate(pairs):
            o = _dot(pb[k * rb:(k + 1) * rb, :], mem_head(mv_ref, bi, hd)) / l[k * rb:(k + 1) * rb, :]
            att_ref[gi * rg + bi * rb:gi * rg + (bi + 1) * rb, hd * dh:(hd + 1) * dh] = o.astype(BF16)
        y = _dot(att_ref[rows, :], wo_ref[...])
        o_ref[rows, :] = _layer_norm(alpha * h + y, g_ref[...], b_ref[...])


def _mem_attn(h, wq, wo, mk, mv, g, b, *, tm, rows_per_batch, n_heads, n_mem, alpha, scale, splits):
    m, d = h.shape
    if tm <= rows_per_batch:
        nb = 1
        steps_per_batch = rows_per_batch // tm
        batch_of = lambda i: i // steps_per_batch
    else:
        nb = tm // rows_per_batch
        batch_of = lambda i: i
    row = pl.BlockSpec((tm, d), lambda i: (i, 0))
    if mk.ndim == 3:
        mem = pl.BlockSpec((nb, n_mem, d), lambda i: (batch_of(i), 0, 0))
    else:
        mem = pl.BlockSpec((nb * n_mem * d // LANES, LANES), lambda i: (batch_of(i), 0))
    return pl.pallas_call(
        functools.partial(_mem_attn_kernel, n_heads=n_heads, n_mem=n_mem, nb=nb, alpha=alpha, scale=scale,
                          splits=splits),
        grid=(m // tm,),
        in_specs=[row, _resident(wq.shape), _resident(wo.shape), mem, mem,
                  _resident(g.shape), _resident(b.shape)],
        out_specs=row, out_shape=jax.ShapeDtypeStruct((m, d), F32),
        scratch_shapes=[pltpu.VMEM((tm, d), BF16)],
        compiler_params=_params("parallel"), name="mem_attn_ln",
    )(h, wq, wo, mk, mv, g, b)


def _ffn_kernel(x_ref, wu_ref, wd_ref, g_ref, b_ref, o_ref, xb_ref, *, alpha):
    f = pl.program_id(1)

    @pl.when(f == 0)
    def _():
        xb_ref[...] = x_ref[...].astype(BF16)
        o_ref[...] = jnp.zeros_like(o_ref)

    a = jnp.maximum(_dot(xb_ref[...], wu_ref[...]), 0.0)
    ab = (a * a).astype(BF16)
    slab = 512
    for n0 in range(0, o_ref.shape[1], slab):
        o_ref[:, n0:n0 + slab] += _dot(ab, wd_ref[:, n0:n0 + slab])

    @pl.when(f == pl.num_programs(1) - 1)
    def _():
        o_ref[...] = _layer_norm(alpha * x_ref[...] + o_ref[...], g_ref[...], b_ref[...])


def _ffn(x, wu, wd, g, b, *, tm, tf, alpha):
    m, d = x.shape
    dff = wu.shape[1]
    row = pl.BlockSpec((tm, d), lambda i, f: (i, 0))
    return pl.pallas_call(
        functools.partial(_ffn_kernel, alpha=alpha), grid=(m // tm, dff // tf),
        in_specs=[row, pl.BlockSpec((d, tf), lambda i, f: (0, f)),
                  pl.BlockSpec((tf, d), lambda i, f: (f, 0)),
                  _resident(g.shape), _resident(b.shape)],
        out_specs=row, out_shape=jax.ShapeDtypeStruct((m, d), F32),
        scratch_shapes=[pltpu.VMEM((tm, d), BF16)],
        compiler_params=_params("parallel", "arbitrary"), name="ffn_ln",
    )(x, wu, wd, g, b)


def _pad_lanes(a):
    return jnp.pad(a, ((0, 0), (0, LANES - a.shape[1])))


def kernel(x_prompt, x_sample, mem_prompt, cache_fox_k, cache_fox_v, cache_fox_logf, cache_mem_k, cache_mem_v, w_in, b_f, sgu_ln_g, sgu_ln_b, w_s, b_s, w_out, ln1_g, ln1_b, w_mq, w_mk, w_mv, w_mo, ln2_g, ln2_b, w_up, w_down, ln3_g, ln3_b):
    depth = w_in.shape[0]
    assert depth == 1
    b, seq, d = x_prompt.shape
    bs_, t_new, _ = x_sample.shape
    past, n_heads, dh = cache_fox_k.shape[2:]
    n_mem, mem_heads, mem_dh = cache_mem_k.shape[2:]
    n_groups, chunk = w_s.shape[1], w_s.shape[2]
    fox_w = n_heads * dh
    gw = d - fox_w
    alpha = (2 * depth) ** 0.25
    l = 0

    wi = w_in[l]
    o3 = 3 * fox_w
    o4 = o3 + n_heads
    wqkv = wi[:, :o3].astype(BF16)
    wf = _pad_lanes(wi[:, o3:o4]).astype(BF16)
    bf = _pad_lanes(b_f[l][None, :])
    wug = wi[:, o4:].astype(BF16)
    lg, lb = sgu_ln_g[l][None, :], sgu_ln_b[l][None, :]
    ws_p, bs_p = w_s[l], b_s[l][:, :, None]
    reps = chunk // t_new
    ws_s = jnp.tile(w_s[l][:, :t_new, :t_new], (1, reps, reps))
    bs_s = jnp.tile(b_s[l][:, :t_new], (1, reps))[:, :, None]
    wo = w_out[l].astype(BF16)
    wmq, wmk, wmv, wmo = (w[l].astype(BF16) for w in (w_mq, w_mk, w_mv, w_mo))
    wu, wd = w_up[l].astype(BF16), w_down[l].astype(BF16)
    ln = [a[l][None, :] for a in (ln1_g, ln1_b, ln2_g, ln2_b, ln3_g, ln3_b)]

    def post(h, fo, go, mk, mv, *, rows_per_batch, tm_attn):
        tm_out = min(1024, h.shape[0])
        h = _out_proj(fo, go, h, wo, ln[0], ln[1], tm=tm_out, alpha=alpha, splits=tm_out // 256)
        h = _mem_attn(h, wmq, wmo, mk, mv, ln[2], ln[3], tm=tm_attn, rows_per_batch=rows_per_batch,
                      n_heads=mem_heads, n_mem=n_mem, alpha=alpha, scale=mem_dh ** -0.5, splits=2)
        return _ffn(h, wu, wd, ln[4], ln[5], tm=512, tf=1024, alpha=alpha)

    xp = x_prompt.reshape(b * seq, d)
    q, k, v, logf, go = _in_proj(xp, wqkv, wf, bf, wug, lg, lb, ws_p, bs_p, tm=512, period=chunk,
                                 n_heads=n_heads, q_scale=dh ** -0.5 * LOG2E, emit_g=False)
    lt = jnp.transpose(logf.reshape(b, seq, n_heads), (0, 2, 1)).reshape(b * n_heads, seq // LANES, LANES)
    c = _cumsum_rows(lt, per_step=n_heads)
    fo = _fox_prompt(q.reshape(b, seq, fox_w), k.reshape(b, seq, fox_w), v.reshape(b, seq, fox_w), c,
                     n_heads=n_heads, hpb=4)
    mem2 = mem_prompt.reshape(b * n_mem, d)
    mkt, mvt, mkb, mvb = _mem_proj(mem2, wmk, wmv, tm=512, n_heads=mem_heads)

    def untile_rows(a):
        a = a.reshape(b, n_mem, mem_dh // LANES, mem_heads, LANES)
        return jnp.transpose(a, (0, 1, 3, 2, 4)).reshape(1, b, n_mem, mem_heads, mem_dh)

    mk, mv = untile_rows(mkt), untile_rows(mvt)
    yp = post(xp, fo.reshape(b * seq, fox_w), go, mkb.reshape(b, n_mem, d), mvb.reshape(b, n_mem, d),
              rows_per_batch=seq, tm_attn=512)

    xs = x_sample.reshape(bs_ * t_new, d)
    qs, ks, vs, logfs, gos, gs = _in_proj(xs, wqkv, wf, bf, wug, lg, lb, ws_s, bs_s, tm=bs_ * t_new,
                                          period=t_new, n_heads=n_heads, q_scale=dh ** -0.5 * LOG2E,
                                          emit_g=True)
    rows_c = -(-(past + t_new) // LANES)
    rows_c = -(-rows_c // 8) * 8
    lcat = jnp.concatenate([cache_fox_logf[l], logfs.reshape(bs_, t_new, n_heads)], axis=1)
    lcat = jnp.pad(lcat, ((0, 0), (0, rows_c * LANES - past - t_new), (0, 0)))
    lts = jnp.transpose(lcat, (0, 2, 1)).reshape(bs_ * n_heads, rows_c, LANES)
    cs = _cumsum_rows(lts, per_step=n_heads).reshape(bs_, n_heads, rows_c * LANES)
    fos = _fox_sample(qs.reshape(bs_, t_new, fox_w), ks.reshape(bs_, t_new, fox_w), vs.reshape(bs_, t_new, fox_w),
                      cache_fox_k.reshape(bs_ * past * n_heads, dh), cache_fox_v.reshape(bs_ * past * n_heads, dh),
                      cs, tk=2048)
    def tile_rows(a):
        a = a.reshape(bs_, n_mem, mem_heads, mem_dh // LANES, LANES)
        return jnp.transpose(a, (0, 1, 3, 2, 4)).reshape(bs_ * n_mem * d // LANES, LANES)

    ys = post(xs, fos.reshape(bs_ * t_new, fox_w), gos, tile_rows(cache_mem_k), tile_rows(cache_mem_v),
              rows_per_batch=t_new, tm_attn=64)

    return (yp.reshape(b, seq, d), ys.reshape(bs_, t_new, d),
            k.reshape(1, b, seq, n_heads, dh), v.reshape(1, b, seq, n_heads, dh),
            logf.reshape(1, b, seq, n_heads),
            mk, mv,
            ks.reshape(1, bs_, t_new, n_heads, dh), vs.reshape(1, bs_, t_new, n_heads, dh),
            logfs.reshape(1, bs_, t_new, n_heads), gs.reshape(1, bs_, t_new, gw))
```

```python
import functools
import math

import jax
import jax.numpy as jnp
from jax import lax
from jax.experimental import pallas as pl
from jax.experimental.pallas import tpu as pltpu

F32 = jnp.float32
BF16 = jnp.bfloat16

LN_EPS = 1e-5
LANES = 128
BF16_ROWS = 16
GELU_C = math.sqrt(2.0 / math.pi)
LOG2E = 1.0 / math.log(2.0)
VMEM_LIMIT = 56 * 1024 * 1024


def _dot(a, b):
    return jnp.dot(a, b, preferred_element_type=F32)


def _dot_nt(a, b):
    return lax.dot_general(a, b, (((1,), (1,)), ((), ())), preferred_element_type=F32)


def _gelu(x):
    return 0.5 * x * (1.0 + jnp.tanh(GELU_C * (x + 0.044715 * (x * x * x))))


def _layer_norm(x, g, b):
    mu = jnp.mean(x, axis=-1, keepdims=True)
    xc = x - mu
    var = jnp.mean(xc * xc, axis=-1, keepdims=True)
    return xc * lax.rsqrt(var + LN_EPS) * g + b


def _log_sigmoid(x):
    return jnp.minimum(x, 0.0) - jnp.log1p(jnp.exp(-jnp.abs(x)))


def _resident(shape):
    nd = len(shape)
    return pl.BlockSpec(shape, lambda *_: (0,) * nd, pipeline_mode=pl.Buffered(1))


def _params(*sem, vmem=VMEM_LIMIT):
    return pltpu.CompilerParams(dimension_semantics=sem, vmem_limit_bytes=vmem)


def _in_proj_kernel(x_ref, wqkv_ref, wf_ref, bf_ref, wug_ref, lg_ref, lb_ref, ws_ref, bs_ref,
                    q_ref, k_ref, v_ref, logf_ref, go_ref, *opt_refs,
                    fox_w, n_heads, gw, n_groups, period, q_scale, emit_kv_bf16, emit_g):
    tm = x_ref.shape[0]
    chunk = ws_ref.shape[1]
    gd = gw // n_groups
    dh = fox_w // n_heads
    opt = list(opt_refs)
    kv_bf16_refs = (opt.pop(0), opt.pop(0)) if emit_kv_bf16 else (None, None)
    xb = x_ref[...].astype(BF16)
    q_ref[...] = (_dot(xb, wqkv_ref[:, 0:fox_w]) * q_scale).astype(BF16)
    for idx, (rows_ref, copy_ref) in enumerate(zip((k_ref, v_ref), kv_bf16_refs)):
        y = _dot(xb, wqkv_ref[:, (idx + 1) * fox_w:(idx + 2) * fox_w])
        for h in range(n_heads):
            rows_ref[pl.ds(h, tm, stride=n_heads), :] = y[:, h * dh:(h + 1) * dh]
        if copy_ref is not None:
            copy_ref[...] = y.astype(BF16)
    zf = _dot(xb, wf_ref[...]) + bf_ref[...]
    logf_ref[...] = _log_sigmoid(zf)[:, :n_heads]
    g = _layer_norm(_gelu(_dot(xb, wug_ref[:, gw:2 * gw])), lg_ref[...], lb_ref[...])
    if emit_g:
        opt.pop(0)[...] = g
    gb = g.astype(BF16)
    u = _gelu(_dot(xb, wug_ref[:, 0:gw]))
    r = lax.broadcasted_iota(jnp.int32, (chunk, chunk), 0)
    c = lax.broadcasted_iota(jnp.int32, (chunk, chunk), 1)
    sh = period.bit_length() - 1
    keep = ((r >> sh) == (c >> sh)) & ((c & (period - 1)) <= (r & (period - 1)))
    for gi in range(n_groups):
        wsg = jnp.where(keep, ws_ref[gi], 0.0).astype(BF16)
        cols = slice(gi * gd, (gi + 1) * gd)
        for ci in range(tm // chunk):
            rows = slice(ci * chunk, (ci + 1) * chunk)
            s = _dot(wsg, gb[rows, cols]) + bs_ref[gi]
            go_ref[rows, cols] = (u[rows, cols] * s).astype(BF16)


def _in_proj(x, wqkv, wf, bf, wug, lg, lb, ws, bs, *, tm, period, n_heads, q_scale, emit_kv_bf16, emit_g):
    m, d = x.shape
    fox_w = wqkv.shape[1] // 3
    gw = wug.shape[1] // 2
    n_groups = ws.shape[0]
    dh = fox_w // n_heads
    row = lambda w: pl.BlockSpec((tm, w), lambda i: (i, 0))
    head_rows = pl.BlockSpec((tm * n_heads, dh), lambda i: (i, 0))
    out_shape = [jax.ShapeDtypeStruct((m, fox_w), BF16),
                 jax.ShapeDtypeStruct((m * n_heads, dh), F32),
                 jax.ShapeDtypeStruct((m * n_heads, dh), F32),
                 jax.ShapeDtypeStruct((m, n_heads), F32),
                 jax.ShapeDtypeStruct((m, gw), BF16)]
    out_specs = [row(fox_w), head_rows, head_rows, row(n_heads), row(gw)]
    if emit_kv_bf16:
        out_shape += [jax.ShapeDtypeStruct((m, fox_w), BF16)] * 2
        out_specs += [row(fox_w)] * 2
    if emit_g:
        out_shape.append(jax.ShapeDtypeStruct((m, gw), F32))
        out_specs.append(row(gw))
    kern = functools.partial(_in_proj_kernel, fox_w=fox_w, n_heads=n_heads, gw=gw, n_groups=n_groups,
                             period=period, q_scale=q_scale, emit_kv_bf16=emit_kv_bf16, emit_g=emit_g)
    return pl.pallas_call(
        kern, grid=(m // tm,),
        in_specs=[row(d), _resident(wqkv.shape), _resident(wf.shape), _resident(bf.shape),
                  _resident(wug.shape), _resident(lg.shape), _resident(lb.shape),
                  _resident(ws.shape), _resident(bs.shape)],
        out_specs=out_specs, out_shape=out_shape,
        compiler_params=_params("parallel"), name="in_proj_sgu",
    )(x, wqkv, wf, bf, wug, lg, lb, ws, bs)


def _cumsum_kernel(x_ref, o_ref, *, rows):
    x = x_ref[...]
    n = x.shape[0]
    hi = lax.Precision.HIGHEST
    ii = lax.broadcasted_iota(jnp.int32, (LANES, LANES), 0)
    jj = lax.broadcasted_iota(jnp.int32, (LANES, LANES), 1)
    within = jnp.dot(x, (ii <= jj).astype(F32), precision=hi, preferred_element_type=F32)
    totals = jnp.dot(x, jnp.ones((LANES, LANES), F32), precision=hi, preferred_element_type=F32)
    ri = lax.broadcasted_iota(jnp.int32, (n, n), 0)
    rj = lax.broadcasted_iota(jnp.int32, (n, n), 1)
    seq_i = jnp.zeros((n, n), jnp.int32)
    seq_j = jnp.zeros((n, n), jnp.int32)
    for s in range(rows, n, rows):
        seq_i += (ri >= s).astype(jnp.int32)
        seq_j += (rj >= s).astype(jnp.int32)
    earlier = ((seq_i == seq_j) & (rj < ri)).astype(F32)
    o_ref[...] = within + jnp.dot(earlier, totals, precision=hi, preferred_element_type=F32)


def _cumsum_rows(x, *, per_step):
    n, rows, _ = x.shape
    x2 = x.reshape(n * rows, LANES)
    spec = pl.BlockSpec((per_step * rows, LANES), lambda i: (i, 0))
    out = pl.pallas_call(
        functools.partial(_cumsum_kernel, rows=rows), grid=(n // per_step,),
        in_specs=[spec], out_specs=spec,
        out_shape=jax.ShapeDtypeStruct(x2.shape, F32),
        compiler_params=_params("parallel"), name="logf_cumsum",
    )(x2)
    return out.reshape(n, rows * LANES)


def _fox_prompt_kernel(q_ref, k_ref, v_ref, c_ref, mask_ref, o_ref, ka_ref, vt_ref, qa_ref, sa_ref, sb_ref,
                       pa_ref, pb_ref, aa_ref, ab_ref, m_ref, acc_ref, *, hk, hpb):
    qi = pl.program_id(2)
    tq = q_ref.shape[1]
    seq = k_ref.shape[1]
    dh = k_ref.shape[2] // hpb
    prep = 512
    ones_rows = (lax.broadcasted_iota(jnp.int32, (BF16_ROWS, prep), 0) == 0).astype(BF16)

    @pl.when(qi == 0)
    def _():
        row = lax.broadcasted_iota(jnp.int32, (dh, prep), 0)
        for hh in range(hpb):
            hc = slice(hh * dh, (hh + 1) * dh)
            for r0 in range(0, seq, prep):
                rs = slice(r0, r0 + prep)
                ka_ref[hh, rs, 0:dh] = k_ref[0, rs, hc]
                vt_ref[hh, 0:dh, rs] = v_ref[0, rs, hc].astype(F32).T.astype(BF16)
                vt_ref[hh, dh:dh + BF16_ROWS, rs] = ones_rows
                c2 = c_ref[0, hh:hh + 1, rs] * LOG2E
                hi = c2.astype(BF16).astype(F32)
                r1 = c2 - hi
                mid = r1.astype(BF16).astype(F32)
                lo = r1 - mid
                aug = jnp.where(row == 0, -hi, jnp.where(row == 1, -mid, jnp.where(row == 2, -lo, 0.0)))
                ka_ref[hh, rs, dh:2 * dh] = aug.T.astype(BF16)

    ones = (lax.broadcasted_iota(jnp.int32, (dh, tq), 0) < 3).astype(BF16)
    for hh in range(hpb):
        qa_ref[hh, 0:dh, :] = q_ref[0, :, hh * dh:(hh + 1) * dh].astype(F32).T.astype(BF16)
        qa_ref[hh, dh:2 * dh, :] = ones

    def qk(hh, blk):
        start = pl.multiple_of(blk * hk, hk)
        return _dot(ka_ref[hh, pl.ds(start, hk), :], qa_ref[hh])

    def softmax(hh, s, p_ref, a_ref):
        m = m_ref[hh]
        m_new = jnp.maximum(m, jnp.max(s, axis=0, keepdims=True))
        p = jnp.exp2(s - m_new)
        a = jnp.exp2(m - m_new)
        m_ref[hh] = m_new
        p_ref[hh] = p.astype(BF16)
        a_ref[hh] = a

    def pv(hh, blk, p_ref, a_ref):
        start = pl.multiple_of(blk * hk, hk)
        acc_ref[hh] = a_ref[hh] * acc_ref[hh] + _dot(vt_ref[hh, :, pl.ds(start, hk)], p_ref[hh])

    for hh in range(hpb):
        m_ref[hh] = jnp.full((1, tq), -jnp.inf, F32)
        acc_ref[hh] = jnp.zeros((dh + BF16_ROWS, tq), F32)
        pb_ref[hh] = jnp.zeros((hk, tq), BF16)
        ab_ref[hh] = jnp.ones((1, tq), F32)
        sa_ref[hh] = qk(hh, 0)

    def body(j, carry):
        for hh in range(hpb):
            sb_ref[hh] = qk(hh, 2 * j + 1)
            pv(hh, jnp.maximum(2 * j - 1, 0), pb_ref, ab_ref)
            softmax(hh, sa_ref[hh], pa_ref, aa_ref)
            sa_ref[hh] = qk(hh, 2 * j + 2)
            pv(hh, 2 * j, pa_ref, aa_ref)
            softmax(hh, sb_ref[hh], pb_ref, ab_ref)
        return carry

    lax.fori_loop(0, qi, body, 0)
    for hh in range(hpb):
        sb_ref[hh] = qk(hh, 2 * qi + 1)
        pv(hh, jnp.maximum(2 * qi - 1, 0), pb_ref, ab_ref)
        softmax(hh, sa_ref[hh] + mask_ref[0], pa_ref, aa_ref)
        pv(hh, 2 * qi, pa_ref, aa_ref)
        softmax(hh, sb_ref[hh] + mask_ref[1], pb_ref, ab_ref)
        pv(hh, 2 * qi + 1, pb_ref, ab_ref)
        out = acc_ref[hh, 0:dh, :] / acc_ref[hh, dh:dh + 1, :]
        o_ref[0, :, hh * dh:(hh + 1) * dh] = out.T.astype(o_ref.dtype)


def _fox_prompt(q, k, v, c, *, n_heads, hpb):
    b, seq, width = q.shape
    dh = width // n_heads
    hk = 2 * LANES
    tq = 2 * hk
    groups = n_heads // hpb
    cg = c.reshape(b * groups, hpb, seq)
    wide = hpb * dh
    key = jnp.arange(2 * hk, dtype=jnp.int32).reshape(2, hk, 1)
    mask = jnp.where(key <= jnp.arange(tq, dtype=jnp.int32)[None, None, :], 0.0, -jnp.inf).astype(F32)
    return pl.pallas_call(
        functools.partial(_fox_prompt_kernel, hk=hk, hpb=hpb),
        grid=(b, groups, seq // tq),
        in_specs=[pl.BlockSpec((1, tq, wide), lambda bi, h, i: (bi, i, h)),
                  pl.BlockSpec((1, seq, wide), lambda bi, h, i: (bi, 0, h)),
                  pl.BlockSpec((1, seq, wide), lambda bi, h, i: (bi, 0, h)),
                  pl.BlockSpec((1, hpb, seq), lambda bi, h, i: (bi * groups + h, 0, 0)),
                  _resident(mask.shape)],
        out_specs=pl.BlockSpec((1, tq, wide), lambda bi, h, i: (bi, i, h)),
        out_shape=jax.ShapeDtypeStruct((b, seq, width), BF16),
        scratch_shapes=[pltpu.VMEM((hpb, seq, 2 * dh), BF16), pltpu.VMEM((hpb, dh + BF16_ROWS, seq), BF16),
                        pltpu.VMEM((hpb, 2 * dh, tq), BF16),
                        pltpu.VMEM((hpb, hk, tq), F32), pltpu.VMEM((hpb, hk, tq), F32),
                        pltpu.VMEM((hpb, hk, tq), BF16), pltpu.VMEM((hpb, hk, tq), BF16),
                        pltpu.VMEM((hpb, 1, tq), F32), pltpu.VMEM((hpb, 1, tq), F32),
                        pltpu.VMEM((hpb, 1, tq), F32),
                        pltpu.VMEM((hpb, dh + BF16_ROWS, tq), F32)],
        compiler_params=_params("parallel", "parallel", "arbitrary"), name="fox_prompt",
    )(q, k, v, cg, mask)


def _fox_sample_kernel(q_ref, kn_ref, vn_ref, kc_ref, vc_ref, c_ref, o_ref, m_ref, l_ref, acc_ref, *, n_heads):
    ci = pl.program_id(1)
    dh = kc_ref.shape[1]
    tk = kc_ref.shape[0] // n_heads
    t_new = q_ref.shape[1]
    past = pl.num_programs(1) * tk

    def update(scores, values):
        s = jnp.concatenate(scores, axis=0)
        m = m_ref[...]
        m_new = jnp.maximum(m, jnp.max(s, axis=-1, keepdims=True))
        p = jnp.exp2(s - m_new)
        a = jnp.exp2(m - m_new)
        m_ref[...] = m_new
        l_ref[...] = a * l_ref[...] + jnp.sum(p, axis=-1, keepdims=True)
        pb = p.astype(BF16)
        pv = [_dot(pb[h * t_new:(h + 1) * t_new, :], values[h]) for h in range(n_heads)]
        acc_ref[...] = a * acc_ref[...] + jnp.concatenate(pv, axis=0)

    heads = [slice(h * dh, (h + 1) * dh) for h in range(n_heads)]

    @pl.when(ci == 0)
    def _():
        m_ref[...] = jnp.full(m_ref.shape, -jnp.inf, F32)
        l_ref[...] = jnp.zeros(l_ref.shape, F32)
        acc_ref[...] = jnp.zeros(acc_ref.shape, F32)
        r = lax.broadcasted_iota(jnp.int32, (t_new, t_new), 0)
        c = lax.broadcasted_iota(jnp.int32, (t_new, t_new), 1)
        scores = []
        for h, hc in enumerate(heads):
            s = _dot_nt(q_ref[0, :, hc], kn_ref[pl.ds(h, t_new, stride=n_heads), :].astype(BF16))
            s = s - c_ref[0, h:h + 1, past:past + t_new] * LOG2E
            scores.append(jnp.where(c <= r, s, -jnp.inf))
        update(scores, [vn_ref[pl.ds(h, t_new, stride=n_heads), :].astype(BF16) for h in range(n_heads)])

    start = pl.multiple_of(ci * tk, tk)
    scores = []
    for h, hc in enumerate(heads):
        s = _dot_nt(q_ref[0, :, hc], kc_ref[pl.ds(h, tk, stride=n_heads), :].astype(BF16))
        scores.append(s - c_ref[0, h:h + 1, pl.ds(start, tk)] * LOG2E)
    update(scores, [vc_ref[pl.ds(h, tk, stride=n_heads), :].astype(BF16) for h in range(n_heads)])

    @pl.when(ci == pl.num_programs(1) - 1)
    def _():
        out = acc_ref[...] / l_ref[...]
        for h, hc in enumerate(heads):
            o_ref[0, :, hc] = out[h * t_new:(h + 1) * t_new, :].astype(o_ref.dtype)


def _fox_sample(q, kn, vn, kc, vc, c, *, tk):
    b, t_new, width = q.shape
    dh = kc.shape[1]
    n_heads = width // dh
    past = kc.shape[0] // (b * n_heads)
    chunks = past // tk
    new = pl.BlockSpec((1, t_new, width), lambda bi, ci: (bi, 0, 0))
    new_rows = pl.BlockSpec((t_new * n_heads, dh), lambda bi, ci: (bi, 0))
    old = pl.BlockSpec((tk * n_heads, dh), lambda bi, ci: (bi * chunks + ci, 0))
    return pl.pallas_call(
        functools.partial(_fox_sample_kernel, n_heads=n_heads), grid=(b, chunks),
        in_specs=[new, new_rows, new_rows, old, old,
                  pl.BlockSpec((1, n_heads, c.shape[2]), lambda bi, ci: (bi, 0, 0))],
        out_specs=new,
        out_shape=jax.ShapeDtypeStruct((b, t_new, width), BF16),
        scratch_shapes=[pltpu.VMEM((n_heads * t_new, 1), F32), pltpu.VMEM((n_heads * t_new, 1), F32),
                        pltpu.VMEM((n_heads * t_new, dh), F32)],
        compiler_params=_params("parallel", "arbitrary"), name="fox_sample",
    )(q, kn, vn, kc, vc, c)


def _out_proj_kernel(fo_ref, go_ref, x_ref, w_ref, g_ref, b_ref, o_ref, *, alpha, splits):
    half = fo_ref.shape[1]
    sub = x_ref.shape[0] // splits
    for r0 in range(0, x_ref.shape[0], sub):
        rows = slice(r0, r0 + sub)
        mix = _dot(fo_ref[rows, :], w_ref[0:half, :]) + _dot(go_ref[rows, :], w_ref[half:2 * half, :])
        o_ref[rows, :] = _layer_norm(alpha * x_ref[rows, :] + mix, g_ref[...], b_ref[...])


def _out_proj(fo, go, x, w, g, b, *, tm, alpha, splits):
    m, d = x.shape
    row = lambda wd: pl.BlockSpec((tm, wd), lambda i: (i, 0))
    return pl.pallas_call(
        functools.partial(_out_proj_kernel, alpha=alpha, splits=splits), grid=(m // tm,),
        in_specs=[row(fo.shape[1]), row(go.shape[1]), row(d),
                  _resident(w.shape), _resident(g.shape), _resident(b.shape)],
        out_specs=row(d), out_shape=jax.ShapeDtypeStruct((m, d), F32),
        compiler_params=_params("parallel"), name="out_proj_ln",
    )(fo, go, x, w, g, b)


def _mem_proj_kernel(x_ref, wk_ref, wv_ref, kt_ref, vt_ref, kb_ref, vb_ref, *, n_heads):
    tm, d = x_ref.shape
    lane_tiles = d // n_heads // LANES
    stride = lane_tiles * n_heads
    xb = x_ref[...].astype(BF16)
    for w_ref, t_ref, b_ref in ((wk_ref, kt_ref, kb_ref), (wv_ref, vt_ref, vb_ref)):
        y = _dot(xb, w_ref[...])
        b_ref[...] = y.astype(BF16)
        for hd in range(n_heads):
            for j in range(lane_tiles):
                c0 = (hd * lane_tiles + j) * LANES
                t_ref[pl.ds(j * n_heads + hd, tm, stride=stride), :] = y[:, c0:c0 + LANES]


def _mem_proj(x, wk, wv, *, tm, n_heads):
    m, d = x.shape
    rows = pl.BlockSpec((tm, d), lambda i: (i, 0))
    tiled = pl.BlockSpec((tm * d // LANES, LANES), lambda i: (i, 0))
    return pl.pallas_call(
        functools.partial(_mem_proj_kernel, n_heads=n_heads), grid=(m // tm,),
        in_specs=[rows, _resident(wk.shape), _resident(wv.shape)],
        out_specs=[tiled, tiled, rows, rows],
        out_shape=[jax.ShapeDtypeStruct((m * d // LANES, LANES), F32)] * 2
        + [jax.ShapeDtypeStruct((m, d), BF16)] * 2,
        compiler_params=_params("parallel"), name="mem_proj",
    )(x, wk, wv)


def _mem_attn_kernel(h_ref, wq_ref, wo_ref, mk_ref, mv_ref, g_ref, b_ref, o_ref, att_ref,
                     *, n_heads, n_mem, nb, alpha, scale, splits):
    tm, d = h_ref.shape
    dh = d // n_heads
    lane_tiles = dh // LANES

    def mem_head(ref, bi, hd):
        if len(ref.shape) == 3:
            return ref[bi, :, hd * dh:(hd + 1) * dh].astype(BF16)
        stride = lane_tiles * n_heads
        parts = [ref[pl.ds(bi * n_mem * stride + j * n_heads + hd, n_mem, stride=stride), :]
                 for j in range(lane_tiles)]
        return jnp.concatenate(parts, axis=1).astype(BF16)

    groups = splits if nb == 1 else 1
    rg = tm // groups
    rb = rg // nb
    for gi in range(groups):
        rows = slice(gi * rg, (gi + 1) * rg)
        h = h_ref[rows, :]
        qb = (_dot(h.astype(BF16), wq_ref[...]) * scale).astype(BF16)
        pairs = [(bi, hd) for bi in range(nb) for hd in range(n_heads)]
        s = jnp.concatenate([_dot_nt(qb[bi * rb:(bi + 1) * rb, hd * dh:(hd + 1) * dh], mem_head(mk_ref, bi, hd))
                             for bi, hd in pairs], axis=0)
        p = jnp.exp(s - jnp.max(s, axis=-1, keepdims=True))
        l = jnp.sum(p, axis=-1, keepdims=True)
        pb = p.astype(BF16)
        for k, (bi, hd) in enumerate(pairs):
            o = _dot(pb[k * rb:(k + 1) * rb, :], mem_head(mv_ref, bi, hd)) / l[k * rb:(k + 1) * rb, :]
            att_ref[gi * rg + bi * rb:gi * rg + (bi + 1) * rb, hd * dh:(hd + 1) * dh] = o.astype(BF16)
        y = _dot(att_ref[rows, :], wo_ref[...])
        o_ref[rows, :] = _layer_norm(alpha * h + y, g_ref[...], b_ref[...])


def _mem_attn(h, wq, wo, mk, mv, g, b, *, tm, rows_per_batch, n_heads, n_mem, alpha, scale, splits):
    m, d = h.shape
    if tm <= rows_per_batch:
        nb = 1
        steps_per_batch = rows_per_batch // tm
        batch_of = lambda i: i // steps_per_batch
    else:
        nb = tm // rows_per_batch
        batch_of = lambda i: i
    row = pl.BlockSpec((tm, d), lambda i: (i, 0))
    if mk.ndim == 3:
        mem = pl.BlockSpec((nb, n_mem, d), lambda i: (batch_of(i), 0, 0))
    else:
        mem = pl.BlockSpec((nb * n_mem * d // LANES, LANES), lambda i: (batch_of(i), 0))
    return pl.pallas_call(
        functools.partial(_mem_attn_kernel, n_heads=n_heads, n_mem=n_mem, nb=nb, alpha=alpha, scale=scale,
                          splits=splits),
        grid=(m // tm,),
        in_specs=[row, _resident(wq.shape), _resident(wo.shape), mem, mem,
                  _resident(g.shape), _resident(b.shape)],
        out_specs=row, out_shape=jax.ShapeDtypeStruct((m, d), F32),
        scratch_shapes=[pltpu.VMEM((tm, d), BF16)],
        compiler_params=_params("parallel"), name="mem_attn_ln",
    )(h, wq, wo, mk, mv, g, b)


def _ffn_kernel(x_ref, wu_ref, wd_ref, g_ref, b_ref, o_ref, xb_ref, *, alpha, groups):
    f = pl.program_id(1)
    last = pl.num_programs(1) - 1
    tm, d = o_ref.shape
    slab = 512

    def hidden(xb):
        a = jnp.maximum(_dot(xb, wu_ref[...]), 0.0)
        return (a * a).astype(BF16)

    @pl.when(f == 0)
    def _():
        xb = x_ref[...].astype(BF16)
        xb_ref[...] = xb
        ab = hidden(xb)
        for n0 in range(0, d, slab):
            o_ref[:, n0:n0 + slab] = _dot(ab, wd_ref[:, n0:n0 + slab])

    @pl.when((f > 0) & (f < last))
    def _():
        ab = hidden(xb_ref[...])
        for n0 in range(0, d, slab):
            o_ref[:, n0:n0 + slab] += _dot(ab, wd_ref[:, n0:n0 + slab])

    @pl.when(f == last)
    def _():
        rg = tm // groups
        for r0 in range(0, tm, rg):
            rows = slice(r0, r0 + rg)
            y = o_ref[rows, :] + _dot(hidden(xb_ref[rows, :]), wd_ref[...])
            o_ref[rows, :] = _layer_norm(alpha * x_ref[rows, :] + y, g_ref[...], b_ref[...])


def _ffn(x, wu, wd, g, b, *, tm, tf, alpha, groups):
    m, d = x.shape
    dff = wu.shape[1]
    assert dff // tf >= 2
    row = pl.BlockSpec((tm, d), lambda i, f: (i, 0))
    return pl.pallas_call(
        functools.partial(_ffn_kernel, alpha=alpha, groups=groups), grid=(m // tm, dff // tf),
        in_specs=[row, pl.BlockSpec((d, tf), lambda i, f: (0, f)),
                  pl.BlockSpec((tf, d), lambda i, f: (f, 0)),
                  _resident(g.shape), _resident(b.shape)],
        out_specs=row, out_shape=jax.ShapeDtypeStruct((m, d), F32),
        scratch_shapes=[pltpu.VMEM((tm, d), BF16)],
        compiler_params=_params("parallel", "arbitrary"), name="ffn_ln",
    )(x, wu, wd, g, b)


def _pad_lanes(a):
    return jnp.pad(a, ((0, 0), (0, LANES - a.shape[1])))


def kernel(x_prompt, x_sample, mem_prompt, cache_fox_k, cache_fox_v, cache_fox_logf, cache_mem_k, cache_mem_v, w_in, b_f, sgu_ln_g, sgu_ln_b, w_s, b_s, w_out, ln1_g, ln1_b, w_mq, w_mk, w_mv, w_mo, ln2_g, ln2_b, w_up, w_down, ln3_g, ln3_b):
    depth = w_in.shape[0]
    assert depth == 1
    b, seq, d = x_prompt.shape
    bs_, t_new, _ = x_sample.shape
    past, n_heads, dh = cache_fox_k.shape[2:]
    n_mem, mem_heads, mem_dh = cache_mem_k.shape[2:]
    n_groups, chunk = w_s.shape[1], w_s.shape[2]
    fox_w = n_heads * dh
    gw = d - fox_w
    alpha = (2 * depth) ** 0.25
    l = 0

    wi = w_in[l]
    o3 = 3 * fox_w
    o4 = o3 + n_heads
    wqkv = wi[:, :o3].astype(BF16)
    wf = _pad_lanes(wi[:, o3:o4]).astype(BF16)
    bf = _pad_lanes(b_f[l][None, :])
    wug = wi[:, o4:].astype(BF16)
    lg, lb = sgu_ln_g[l][None, :], sgu_ln_b[l][None, :]
    ws_p, bs_p = w_s[l], b_s[l][:, :, None]
    reps = chunk // t_new
    ws_s = jnp.tile(w_s[l][:, :t_new, :t_new], (1, reps, reps))
    bs_s = jnp.tile(b_s[l][:, :t_new], (1, reps))[:, :, None]
    wo = w_out[l].astype(BF16)
    wmq, wmk, wmv, wmo = (w[l].astype(BF16) for w in (w_mq, w_mk, w_mv, w_mo))
    wu, wd = w_up[l].astype(BF16), w_down[l].astype(BF16)
    ln = [a[l][None, :] for a in (ln1_g, ln1_b, ln2_g, ln2_b, ln3_g, ln3_b)]

    def post(h, fo, go, mk, mv, *, rows_per_batch, tm_attn):
        tm_out = min(1024, h.shape[0])
        h = _out_proj(fo, go, h, wo, ln[0], ln[1], tm=tm_out, alpha=alpha, splits=tm_out // 256)
        h = _mem_attn(h, wmq, wmo, mk, mv, ln[2], ln[3], tm=tm_attn, rows_per_batch=rows_per_batch,
                      n_heads=mem_heads, n_mem=n_mem, alpha=alpha, scale=mem_dh ** -0.5, splits=2)
        return _ffn(h, wu, wd, ln[4], ln[5], tm=512, tf=1024, alpha=alpha, groups=2)

    xp = x_prompt.reshape(b * seq, d)
    q, k, v, logf, go, kb, vb = _in_proj(xp, wqkv, wf, bf, wug, lg, lb, ws_p, bs_p, tm=512, period=chunk,
                                         n_heads=n_heads, q_scale=dh ** -0.5 * LOG2E,
                                         emit_kv_bf16=True, emit_g=False)
    lt = jnp.transpose(logf.reshape(b, seq, n_heads), (0, 2, 1)).reshape(b * n_heads, seq // LANES, LANES)
    c = _cumsum_rows(lt, per_step=n_heads)
    fo = _fox_prompt(q.reshape(b, seq, fox_w), kb.reshape(b, seq, fox_w), vb.reshape(b, seq, fox_w), c,
                     n_heads=n_heads, hpb=4)
    mem2 = mem_prompt.reshape(b * n_mem, d)
    mkt, mvt, mkb, mvb = _mem_proj(mem2, wmk, wmv, tm=512, n_heads=mem_heads)

    def untile_rows(a):
        a = a.reshape(b, n_mem, mem_dh // LANES, mem_heads, LANES)
        return jnp.transpose(a, (0, 1, 3, 2, 4)).reshape(1, b, n_mem, mem_heads, mem_dh)

    mk, mv = untile_rows(mkt), untile_rows(mvt)
    yp = post(xp, fo.reshape(b * seq, fox_w), go, mkb.reshape(b, n_mem, d), mvb.reshape(b, n_mem, d),
              rows_per_batch=seq, tm_attn=512)

    xs = x_sample.reshape(bs_ * t_new, d)
    qs, ks, vs, logfs, gos, gs = _in_proj(xs, wqkv, wf, bf, wug, lg, lb, ws_s, bs_s, tm=bs_ * t_new,
                                          period=t_new, n_heads=n_heads, q_scale=dh ** -0.5 * LOG2E,
                                          emit_kv_bf16=False, emit_g=True)
    rows_c = -(-(past + t_new) // LANES)
    rows_c = -(-rows_c // 8) * 8
    lcat = jnp.concatenate([cache_fox_logf[l], logfs.reshape(bs_, t_new, n_heads)], axis=1)
    lcat = jnp.pad(lcat, ((0, 0), (0, rows_c * LANES - past - t_new), (0, 0)))
    lts = jnp.transpose(lcat, (0, 2, 1)).reshape(bs_ * n_heads, rows_c, LANES)
    cs = _cumsum_rows(lts, per_step=n_heads).reshape(bs_, n_heads, rows_c * LANES)
    fos = _fox_sample(qs.reshape(bs_, t_new, fox_w), ks, vs,
                      cache_fox_k.reshape(bs_ * past * n_heads, dh), cache_fox_v.reshape(bs_ * past * n_heads, dh),
                      cs, tk=2048)
    def tile_rows(a):
        a = a.reshape(bs_, n_mem, mem_heads, mem_dh // LANES, LANES)
        return jnp.transpose(a, (0, 1, 3, 2, 4)).reshape(bs_ * n_mem * d // LANES, LANES)

    ys = post(xs, fos.reshape(bs_ * t_new, fox_w), gos, tile_rows(cache_mem_k), tile_rows(cache_mem_v),
              rows_per_batch=t_new, tm_attn=64)

    return (yp.reshape(b, seq, d), ys.reshape(bs_, t_new, d),
            k.reshape(1, b, seq, n_heads, dh), v.reshape(1, b, seq, n_heads, dh),
            logf.reshape(1, b, seq, n_heads),
            mk, mv,
            ks.reshape(1, bs_, t_new, n_heads, dh), vs.reshape(1, bs_, t_new, n_heads, dh),
            logfs.reshape(1, bs_, t_new, n_heads), gs.reshape(1, bs_, t_new, gw))
```

```python
import functools
import math

import jax
import jax.numpy as jnp
from jax import lax
from jax.experimental import pallas as pl
from jax.experimental.pallas import tpu as pltpu

F32 = jnp.float32
BF16 = jnp.bfloat16

LN_EPS = 1e-5
LANES = 128
BF16_ROWS = 16
GELU_C = math.sqrt(2.0 / math.pi)
LOG2E = 1.0 / math.log(2.0)
VMEM_LIMIT = 56 * 1024 * 1024

ROW_TILE = 512
ROW_GROUP = 256
OUT_PROJ_TILE = 1024
FFN_CHUNK = 1024
FFN_SLAB = 512
FOX_KEY_BLOCK = 2 * LANES
FOX_HEADS_PER_STEP = 4
FOX_PREP_ROWS = 512
FOX_SAMPLE_KEYS = 2048
MEM_ATTN_SAMPLE_ROWS = 128


def _dot(a, b):
    return jnp.dot(a, b, preferred_element_type=F32)


def _dot_nt(a, b):
    return lax.dot_general(a, b, (((1,), (1,)), ((), ())), preferred_element_type=F32)


def _gelu(x):
    return 0.5 * x * (1.0 + jnp.tanh(GELU_C * (x + 0.044715 * (x * x * x))))


def _layer_norm(x, g, b):
    mu = jnp.mean(x, axis=-1, keepdims=True)
    xc = x - mu
    var = jnp.mean(xc * xc, axis=-1, keepdims=True)
    return xc * lax.rsqrt(var + LN_EPS) * g + b


def _log_sigmoid(x):
    return jnp.minimum(x, 0.0) - jnp.log1p(jnp.exp(-jnp.abs(x)))


def _resident(shape):
    nd = len(shape)
    return pl.BlockSpec(shape, lambda *_: (0,) * nd, pipeline_mode=pl.Buffered(1))


def _params(*sem, vmem=VMEM_LIMIT):
    return pltpu.CompilerParams(dimension_semantics=sem, vmem_limit_bytes=vmem)


def _in_proj_kernel(x_ref, wqkv_ref, wf_ref, bf_ref, wug_ref, lg_ref, lb_ref, ws_ref, bs_ref,
                    q_ref, k_ref, v_ref, logf_ref, go_ref, *opt_refs,
                    fox_w, n_heads, gw, n_groups, period, q_scale, emit_kv_bf16, emit_g):
    tm = x_ref.shape[0]
    chunk = ws_ref.shape[1]
    gd = gw // n_groups
    dh = fox_w // n_heads
    opt = list(opt_refs)
    kv_bf16_refs = (opt.pop(0), opt.pop(0)) if emit_kv_bf16 else (None, None)
    xb = x_ref[...].astype(BF16)
    q_ref[...] = (_dot(xb, wqkv_ref[:, 0:fox_w]) * q_scale).astype(BF16)
    for idx, (rows_ref, copy_ref) in enumerate(zip((k_ref, v_ref), kv_bf16_refs)):
        y = _dot(xb, wqkv_ref[:, (idx + 1) * fox_w:(idx + 2) * fox_w])
        for h in range(n_heads):
            rows_ref[pl.ds(h, tm, stride=n_heads), :] = y[:, h * dh:(h + 1) * dh]
        if copy_ref is not None:
            copy_ref[...] = y.astype(BF16)
    zf = _dot(xb, wf_ref[...]) + bf_ref[...]
    logf_ref[...] = _log_sigmoid(zf)[:, :n_heads]
    g = _layer_norm(_gelu(_dot(xb, wug_ref[:, gw:2 * gw])), lg_ref[...], lb_ref[...])
    if emit_g:
        opt.pop(0)[...] = g
    gb = g.astype(BF16)
    u = _gelu(_dot(xb, wug_ref[:, 0:gw]))
    r = lax.broadcasted_iota(jnp.int32, (chunk, chunk), 0)
    c = lax.broadcasted_iota(jnp.int32, (chunk, chunk), 1)
    sh = period.bit_length() - 1
    keep = ((r >> sh) == (c >> sh)) & ((c & (period - 1)) <= (r & (period - 1)))
    for gi in range(n_groups):
        wsg = jnp.where(keep, ws_ref[gi], 0.0).astype(BF16)
        cols = slice(gi * gd, (gi + 1) * gd)
        for ci in range(tm // chunk):
            rows = slice(ci * chunk, (ci + 1) * chunk)
            s = _dot(wsg, gb[rows, cols]) + bs_ref[gi]
            go_ref[rows, cols] = (u[rows, cols] * s).astype(BF16)


def _in_proj(x, wqkv, wf, bf, wug, lg, lb, ws, bs, *, tm, period, n_heads, q_scale, emit_kv_bf16, emit_g):
    m, d = x.shape
    fox_w = wqkv.shape[1] // 3
    gw = wug.shape[1] // 2
    n_groups = ws.shape[0]
    dh = fox_w // n_heads
    row = lambda w: pl.BlockSpec((tm, w), lambda i: (i, 0))
    head_rows = pl.BlockSpec((tm * n_heads, dh), lambda i: (i, 0))
    out_shape = [jax.ShapeDtypeStruct((m, fox_w), BF16),
                 jax.ShapeDtypeStruct((m * n_heads, dh), F32),
                 jax.ShapeDtypeStruct((m * n_heads, dh), F32),
                 jax.ShapeDtypeStruct((m, n_heads), F32),
                 jax.ShapeDtypeStruct((m, gw), BF16)]
    out_specs = [row(fox_w), head_rows, head_rows, row(n_heads), row(gw)]
    if emit_kv_bf16:
        out_shape += [jax.ShapeDtypeStruct((m, fox_w), BF16)] * 2
        out_specs += [row(fox_w)] * 2
    if emit_g:
        out_shape.append(jax.ShapeDtypeStruct((m, gw), F32))
        out_specs.append(row(gw))
    kern = functools.partial(_in_proj_kernel, fox_w=fox_w, n_heads=n_heads, gw=gw, n_groups=n_groups,
                             period=period, q_scale=q_scale, emit_kv_bf16=emit_kv_bf16, emit_g=emit_g)
    return pl.pallas_call(
        kern, grid=(m // tm,),
        in_specs=[row(d), _resident(wqkv.shape), _resident(wf.shape), _resident(bf.shape),
                  _resident(wug.shape), _resident(lg.shape), _resident(lb.shape),
                  _resident(ws.shape), _resident(bs.shape)],
        out_specs=out_specs, out_shape=out_shape,
        compiler_params=_params("parallel"), name="in_proj_sgu",
    )(x, wqkv, wf, bf, wug, lg, lb, ws, bs)


def _cumsum_kernel(x_ref, o_ref, *, rows):
    x = x_ref[...]
    n = x.shape[0]
    hi = lax.Precision.HIGHEST
    ii = lax.broadcasted_iota(jnp.int32, (LANES, LANES), 0)
    jj = lax.broadcasted_iota(jnp.int32, (LANES, LANES), 1)
    within = jnp.dot(x, (ii <= jj).astype(F32), precision=hi, preferred_element_type=F32)
    totals = jnp.dot(x, jnp.ones((LANES, LANES), F32), precision=hi, preferred_element_type=F32)
    ri = lax.broadcasted_iota(jnp.int32, (n, n), 0)
    rj = lax.broadcasted_iota(jnp.int32, (n, n), 1)
    seq_i = jnp.zeros((n, n), jnp.int32)
    seq_j = jnp.zeros((n, n), jnp.int32)
    for s in range(rows, n, rows):
        seq_i += (ri >= s).astype(jnp.int32)
        seq_j += (rj >= s).astype(jnp.int32)
    earlier = ((seq_i == seq_j) & (rj < ri)).astype(F32)
    o_ref[...] = within + jnp.dot(earlier, totals, precision=hi, preferred_element_type=F32)


def _cumsum_rows(x, *, per_step):
    n, rows, _ = x.shape
    x2 = x.reshape(n * rows, LANES)
    spec = pl.BlockSpec((per_step * rows, LANES), lambda i: (i, 0))
    out = pl.pallas_call(
        functools.partial(_cumsum_kernel, rows=rows), grid=(n // per_step,),
        in_specs=[spec], out_specs=spec,
        out_shape=jax.ShapeDtypeStruct(x2.shape, F32),
        compiler_params=_params("parallel"), name="logf_cumsum",
    )(x2)
    return out.reshape(n, rows * LANES)


def _fox_prompt_kernel(q_ref, k_ref, v_ref, c_ref, mask_ref, o_ref, ka_ref, vt_ref, qa_ref, sa_ref, sb_ref,
                       pa_ref, pb_ref, aa_ref, ab_ref, m_ref, acc_ref, *, hk, hpb):
    qi = pl.program_id(2)
    tq = q_ref.shape[1]
    seq = k_ref.shape[1]
    dh = k_ref.shape[2] // hpb
    prep = FOX_PREP_ROWS
    ones_rows = (lax.broadcasted_iota(jnp.int32, (BF16_ROWS, prep), 0) == 0).astype(BF16)

    @pl.when(qi == 0)
    def _():
        row = lax.broadcasted_iota(jnp.int32, (dh, prep), 0)
        for hh in range(hpb):
            hc = slice(hh * dh, (hh + 1) * dh)
            for r0 in range(0, seq, prep):
                rs = slice(r0, r0 + prep)
                ka_ref[hh, rs, 0:dh] = k_ref[0, rs, hc]
                vt_ref[hh, 0:dh, rs] = v_ref[0, rs, hc].astype(F32).T.astype(BF16)
                vt_ref[hh, dh:dh + BF16_ROWS, rs] = ones_rows
                c2 = c_ref[0, hh:hh + 1, rs] * LOG2E
                hi = c2.astype(BF16).astype(F32)
                r1 = c2 - hi
                mid = r1.astype(BF16).astype(F32)
                lo = r1 - mid
                aug = jnp.where(row == 0, -hi, jnp.where(row == 1, -mid, jnp.where(row == 2, -lo, 0.0)))
                ka_ref[hh, rs, dh:2 * dh] = aug.T.astype(BF16)

    ones = (lax.broadcasted_iota(jnp.int32, (dh, tq), 0) < 3).astype(BF16)
    for hh in range(hpb):
        qa_ref[hh, 0:dh, :] = q_ref[0, :, hh * dh:(hh + 1) * dh].astype(F32).T.astype(BF16)
        qa_ref[hh, dh:2 * dh, :] = ones

    def qk(hh, blk):
        start = pl.multiple_of(blk * hk, hk)
        return _dot(ka_ref[hh, pl.ds(start, hk), :], qa_ref[hh])

    def softmax(hh, s, p_ref, a_ref):
        m = m_ref[hh]
        m_new = jnp.maximum(m, jnp.max(s, axis=0, keepdims=True))
        p = jnp.exp2(s - m_new)
        a = jnp.exp2(m - m_new)
        m_ref[hh] = m_new
        p_ref[hh] = p.astype(BF16)
        a_ref[hh] = a

    def pv(hh, blk, p_ref, a_ref):
        start = pl.multiple_of(blk * hk, hk)
        acc_ref[hh] = a_ref[hh] * acc_ref[hh] + _dot(vt_ref[hh, :, pl.ds(start, hk)], p_ref[hh])

    for hh in range(hpb):
        m_ref[hh] = jnp.full((1, tq), -jnp.inf, F32)
        acc_ref[hh] = jnp.zeros((dh + BF16_ROWS, tq), F32)
        pb_ref[hh] = jnp.zeros((hk, tq), BF16)
        ab_ref[hh] = jnp.ones((1, tq), F32)
        sa_ref[hh] = qk(hh, 0)

    heads = range(hpb)

    def body(j, carry):
        for hh in heads:
            sb_ref[hh] = qk(hh, 2 * j + 1)
            pv(hh, jnp.maximum(2 * j - 1, 0), pb_ref, ab_ref)
            softmax(hh, sa_ref[hh], pa_ref, aa_ref)
            sa_ref[hh] = qk(hh, 2 * j + 2)
            pv(hh, 2 * j, pa_ref, aa_ref)
            softmax(hh, sb_ref[hh], pb_ref, ab_ref)
        return carry

    lax.fori_loop(0, qi, body, 0)
    for hh in heads:
        sb_ref[hh] = qk(hh, 2 * qi + 1)
    for hh in heads:
        pv(hh, jnp.maximum(2 * qi - 1, 0), pb_ref, ab_ref)
    for hh in heads:
        softmax(hh, sa_ref[hh] + mask_ref[0], pa_ref, aa_ref)
    for hh in heads:
        pv(hh, 2 * qi, pa_ref, aa_ref)
    for hh in heads:
        softmax(hh, sb_ref[hh] + mask_ref[1], pb_ref, ab_ref)
    for hh in heads:
        pv(hh, 2 * qi + 1, pb_ref, ab_ref)
    for hh in heads:
        out = acc_ref[hh, 0:dh, :] / acc_ref[hh, dh:dh + 1, :]
        o_ref[0, :, hh * dh:(hh + 1) * dh] = out.T.astype(o_ref.dtype)


def _fox_prompt(q, k, v, c, *, n_heads, hpb):
    b, seq, width = q.shape
    dh = width // n_heads
    hk = FOX_KEY_BLOCK
    tq = 2 * hk
    groups = n_heads // hpb
    cg = c.reshape(b * groups, hpb, seq)
    wide = hpb * dh
    key = jnp.arange(2 * hk, dtype=jnp.int32).reshape(2, hk, 1)
    mask = jnp.where(key <= jnp.arange(tq, dtype=jnp.int32)[None, None, :], 0.0, -jnp.inf).astype(F32)
    return pl.pallas_call(
        functools.partial(_fox_prompt_kernel, hk=hk, hpb=hpb),
        grid=(b, groups, seq // tq),
        in_specs=[pl.BlockSpec((1, tq, wide), lambda bi, h, i: (bi, i, h)),
                  pl.BlockSpec((1, seq, wide), lambda bi, h, i: (bi, 0, h)),
                  pl.BlockSpec((1, seq, wide), lambda bi, h, i: (bi, 0, h)),
                  pl.BlockSpec((1, hpb, seq), lambda bi, h, i: (bi * groups + h, 0, 0)),
                  _resident(mask.shape)],
        out_specs=pl.BlockSpec((1, tq, wide), lambda bi, h, i: (bi, i, h)),
        out_shape=jax.ShapeDtypeStruct((b, seq, width), BF16),
        scratch_shapes=[pltpu.VMEM((hpb, seq, 2 * dh), BF16), pltpu.VMEM((hpb, dh + BF16_ROWS, seq), BF16),
                        pltpu.VMEM((hpb, 2 * dh, tq), BF16),
                        pltpu.VMEM((hpb, hk, tq), F32), pltpu.VMEM((hpb, hk, tq), F32),
                        pltpu.VMEM((hpb, hk, tq), BF16), pltpu.VMEM((hpb, hk, tq), BF16),
                        pltpu.VMEM((hpb, 1, tq), F32), pltpu.VMEM((hpb, 1, tq), F32),
                        pltpu.VMEM((hpb, 1, tq), F32),
                        pltpu.VMEM((hpb, dh + BF16_ROWS, tq), F32)],
        compiler_params=_params("parallel", "parallel", "arbitrary"), name="fox_prompt",
    )(q, k, v, cg, mask)


def _fox_sample_kernel(q_ref, kn_ref, vn_ref, kc_ref, vc_ref, c_ref, o_ref, m_ref, l_ref, acc_ref, *, n_heads):
    ci = pl.program_id(1)
    dh = kc_ref.shape[1]
    tk = kc_ref.shape[0] // n_heads
    t_new = q_ref.shape[1]
    past = pl.num_programs(1) * tk

    def update(scores, values):
        s = jnp.concatenate(scores, axis=0)
        m = m_ref[...]
        m_new = jnp.maximum(m, jnp.max(s, axis=-1, keepdims=True))
        p = jnp.exp2(s - m_new)
        a = jnp.exp2(m - m_new)
        m_ref[...] = m_new
        l_ref[...] = a * l_ref[...] + jnp.sum(p, axis=-1, keepdims=True)
        pb = p.astype(BF16)
        pv = [_dot(pb[h * t_new:(h + 1) * t_new, :], values[h]) for h in range(n_heads)]
        acc_ref[...] = a * acc_ref[...] + jnp.concatenate(pv, axis=0)

    heads = [slice(h * dh, (h + 1) * dh) for h in range(n_heads)]

    @pl.when(ci == 0)
    def _():
        m_ref[...] = jnp.full(m_ref.shape, -jnp.inf, F32)
        l_ref[...] = jnp.zeros(l_ref.shape, F32)
        acc_ref[...] = jnp.zeros(acc_ref.shape, F32)
        r = lax.broadcasted_iota(jnp.int32, (t_new, t_new), 0)
        c = lax.broadcasted_iota(jnp.int32, (t_new, t_new), 1)
        scores = []
        for h, hc in enumerate(heads):
            s = _dot_nt(q_ref[0, :, hc], kn_ref[pl.ds(h, t_new, stride=n_heads), :].astype(BF16))
            s = s - c_ref[0, h:h + 1, past:past + t_new] * LOG2E
            scores.append(jnp.where(c <= r, s, -jnp.inf))
        update(scores, [vn_ref[pl.ds(h, t_new, stride=n_heads), :].astype(BF16) for h in range(n_heads)])

    start = pl.multiple_of(ci * tk, tk)
    scores = []
    for h, hc in enumerate(heads):
        s = _dot_nt(q_ref[0, :, hc], kc_ref[pl.ds(h, tk, stride=n_heads), :].astype(BF16))
        scores.append(s - c_ref[0, h:h + 1, pl.ds(start, tk)] * LOG2E)
    update(scores, [vc_ref[pl.ds(h, tk, stride=n_heads), :].astype(BF16) for h in range(n_heads)])

    @pl.when(ci == pl.num_programs(1) - 1)
    def _():
        out = acc_ref[...] / l_ref[...]
        for h, hc in enumerate(heads):
            o_ref[0, :, hc] = out[h * t_new:(h + 1) * t_new, :].astype(o_ref.dtype)


def _fox_sample(q, kn, vn, kc, vc, c, *, tk):
    b, t_new, width = q.shape
    dh = kc.shape[1]
    n_heads = width // dh
    past = kc.shape[0] // (b * n_heads)
    chunks = past // tk
    new = pl.BlockSpec((1, t_new, width), lambda bi, ci: (bi, 0, 0))
    new_rows = pl.BlockSpec((t_new * n_heads, dh), lambda bi, ci: (bi, 0))
    old = pl.BlockSpec((tk * n_heads, dh), lambda bi, ci: (bi * chunks + ci, 0))
    return pl.pallas_call(
        functools.partial(_fox_sample_kernel, n_heads=n_heads), grid=(b, chunks),
        in_specs=[new, new_rows, new_rows, old, old,
                  pl.BlockSpec((1, n_heads, c.shape[2]), lambda bi, ci: (bi, 0, 0))],
        out_specs=new,
        out_shape=jax.ShapeDtypeStruct((b, t_new, width), BF16),
        scratch_shapes=[pltpu.VMEM((n_heads * t_new, 1), F32), pltpu.VMEM((n_heads * t_new, 1), F32),
                        pltpu.VMEM((n_heads * t_new, dh), F32)],
        compiler_params=_params("parallel", "arbitrary"), name="fox_sample",
    )(q, kn, vn, kc, vc, c)


def _out_proj_kernel(fo_ref, go_ref, x_ref, w_ref, g_ref, b_ref, o_ref, *, alpha, splits):
    half = fo_ref.shape[1]
    sub = x_ref.shape[0] // splits
    for r0 in range(0, x_ref.shape[0], sub):
        rows = slice(r0, r0 + sub)
        mix = _dot(fo_ref[rows, :], w_ref[0:half, :]) + _dot(go_ref[rows, :], w_ref[half:2 * half, :])
        o_ref[rows, :] = _layer_norm(alpha * x_ref[rows, :] + mix, g_ref[...], b_ref[...])


def _out_proj(fo, go, x, w, g, b, *, tm, alpha, splits):
    m, d = x.shape
    row = lambda wd: pl.BlockSpec((tm, wd), lambda i: (i, 0))
    return pl.pallas_call(
        functools.partial(_out_proj_kernel, alpha=alpha, splits=splits), grid=(m // tm,),
        in_specs=[row(fo.shape[1]), row(go.shape[1]), row(d),
                  _resident(w.shape), _resident(g.shape), _resident(b.shape)],
        out_specs=row(d), out_shape=jax.ShapeDtypeStruct((m, d), F32),
        compiler_params=_params("parallel"), name="out_proj_ln",
    )(fo, go, x, w, g, b)


def _mem_proj_kernel(x_ref, wk_ref, wv_ref, kt_ref, vt_ref, kb_ref, vb_ref, *, n_heads):
    tm, d = x_ref.shape
    lane_tiles = d // n_heads // LANES
    stride = lane_tiles * n_heads
    xb = x_ref[...].astype(BF16)
    for w_ref, t_ref, b_ref in ((wk_ref, kt_ref, kb_ref), (wv_ref, vt_ref, vb_ref)):
        y = _dot(xb, w_ref[...])
        b_ref[...] = y.astype(BF16)
        for hd in range(n_heads):
            for j in range(lane_tiles):
                c0 = (hd * lane_tiles + j) * LANES
                t_ref[pl.ds(j * n_heads + hd, tm, stride=stride), :] = y[:, c0:c0 + LANES]


def _mem_proj(x, wk, wv, *, tm, n_heads):
    m, d = x.shape
    rows = pl.BlockSpec((tm, d), lambda i: (i, 0))
    tiled = pl.BlockSpec((tm * d // LANES, LANES), lambda i: (i, 0))
    return pl.pallas_call(
        functools.partial(_mem_proj_kernel, n_heads=n_heads), grid=(m // tm,),
        in_specs=[rows, _resident(wk.shape), _resident(wv.shape)],
        out_specs=[tiled, tiled, rows, rows],
        out_shape=[jax.ShapeDtypeStruct((m * d // LANES, LANES), F32)] * 2
        + [jax.ShapeDtypeStruct((m, d), BF16)] * 2,
        compiler_params=_params("parallel"), name="mem_proj",
    )(x, wk, wv)


def _mem_attn_kernel(h_ref, wq_ref, wo_ref, mk_ref, mv_ref, g_ref, b_ref, o_ref, att_ref,
                     *, n_heads, n_mem, nb, alpha, scale, splits):
    tm, d = h_ref.shape
    dh = d // n_heads
    lane_tiles = dh // LANES

    def mem_head(ref, bi, hd):
        if len(ref.shape) == 3:
            return ref[bi, :, hd * dh:(hd + 1) * dh].astype(BF16)
        stride = lane_tiles * n_heads
        parts = [ref[pl.ds(bi * n_mem * stride + j * n_heads + hd, n_mem, stride=stride), :]
                 for j in range(lane_tiles)]
        return jnp.concatenate(parts, axis=1).astype(BF16)

    groups = splits if nb == 1 else 1
    rg = tm // groups
    rb = rg // nb
    pairs = [(bi, hd) for bi in range(nb) for hd in range(n_heads)]
    group_rows = [slice(gi * rg, (gi + 1) * rg) for gi in range(groups)]
    qbs = [(_dot(h_ref[rows, :].astype(BF16), wq_ref[...]) * scale).astype(BF16) for rows in group_rows]
    for gi, qb in enumerate(qbs):
        s = jnp.concatenate([_dot_nt(qb[bi * rb:(bi + 1) * rb, hd * dh:(hd + 1) * dh], mem_head(mk_ref, bi, hd))
                             for bi, hd in pairs], axis=0)
        p = jnp.exp(s - jnp.max(s, axis=-1, keepdims=True))
        l = jnp.sum(p, axis=-1, keepdims=True)
        pb = p.astype(BF16)
        for k, (bi, hd) in enumerate(pairs):
            o = _dot(pb[k * rb:(k + 1) * rb, :], mem_head(mv_ref, bi, hd)) / l[k * rb:(k + 1) * rb, :]
            att_ref[gi * rg + bi * rb:gi * rg + (bi + 1) * rb, hd * dh:(hd + 1) * dh] = o.astype(BF16)
    ys = [_dot(att_ref[rows, :], wo_ref[...]) for rows in group_rows]
    for rows, y in zip(group_rows, ys):
        o_ref[rows, :] = _layer_norm(alpha * h_ref[rows, :] + y, g_ref[...], b_ref[...])


def _mem_attn(h, wq, wo, mk, mv, g, b, *, tm, rows_per_batch, n_heads, n_mem, alpha, scale, splits):
    m, d = h.shape
    if tm <= rows_per_batch:
        nb = 1
        steps_per_batch = rows_per_batch // tm
        batch_of = lambda i: i // steps_per_batch
    else:
        nb = tm // rows_per_batch
        batch_of = lambda i: i
    row = pl.BlockSpec((tm, d), lambda i: (i, 0))
    if mk.ndim == 3:
        mem = pl.BlockSpec((nb, n_mem, d), lambda i: (batch_of(i), 0, 0))
    else:
        mem = pl.BlockSpec((nb * n_mem * d // LANES, LANES), lambda i: (batch_of(i), 0))
    return pl.pallas_call(
        functools.partial(_mem_attn_kernel, n_heads=n_heads, n_mem=n_mem, nb=nb, alpha=alpha, scale=scale,
                          splits=splits),
        grid=(m // tm,),
        in_specs=[row, _resident(wq.shape), _resident(wo.shape), mem, mem,
                  _resident(g.shape), _resident(b.shape)],
        out_specs=row, out_shape=jax.ShapeDtypeStruct((m, d), F32),
        scratch_shapes=[pltpu.VMEM((tm, d), BF16)],
        compiler_params=_params("parallel"), name="mem_attn_ln",
    )(h, wq, wo, mk, mv, g, b)


def _ffn_kernel(x_ref, wu_ref, wd_ref, g_ref, b_ref, o_ref, xb_ref, *, alpha, groups):
    f = pl.program_id(1)
    last = pl.num_programs(1) - 1
    tm, d = o_ref.shape
    slab = FFN_SLAB

    def hidden(xb):
        a = jnp.maximum(_dot(xb, wu_ref[...]), 0.0)
        return (a * a).astype(BF16)

    @pl.when(f == 0)
    def _():
        xb = x_ref[...].astype(BF16)
        xb_ref[...] = xb
        ab = hidden(xb)
        for n0 in range(0, d, slab):
            o_ref[:, n0:n0 + slab] = _dot(ab, wd_ref[:, n0:n0 + slab])

    @pl.when((f > 0) & (f < last))
    def _():
        ab = hidden(xb_ref[...])
        for n0 in range(0, d, slab):
            o_ref[:, n0:n0 + slab] += _dot(ab, wd_ref[:, n0:n0 + slab])

    @pl.when(f == last)
    def _():
        rg = tm // groups
        for r0 in range(0, tm, rg):
            rows = slice(r0, r0 + rg)
            y = o_ref[rows, :] + _dot(hidden(xb_ref[rows, :]), wd_ref[...])
            o_ref[rows, :] = _layer_norm(alpha * x_ref[rows, :] + y, g_ref[...], b_ref[...])


def _ffn(x, wu, wd, g, b, *, tm, tf, alpha, groups):
    m, d = x.shape
    dff = wu.shape[1]
    assert dff // tf >= 2
    row = pl.BlockSpec((tm, d), lambda i, f: (i, 0))
    return pl.pallas_call(
        functools.partial(_ffn_kernel, alpha=alpha, groups=groups), grid=(m // tm, dff // tf),
        in_specs=[row, pl.BlockSpec((d, tf), lambda i, f: (0, f)),
                  pl.BlockSpec((tf, d), lambda i, f: (f, 0)),
                  _resident(g.shape), _resident(b.shape)],
        out_specs=row, out_shape=jax.ShapeDtypeStruct((m, d), F32),
        scratch_shapes=[pltpu.VMEM((tm, d), BF16)],
        compiler_params=_params("parallel", "arbitrary"), name="ffn_ln",
    )(x, wu, wd, g, b)


def _pad_lanes(a):
    return jnp.pad(a, ((0, 0), (0, LANES - a.shape[1])))


def kernel(x_prompt, x_sample, mem_prompt, cache_fox_k, cache_fox_v, cache_fox_logf, cache_mem_k, cache_mem_v, w_in, b_f, sgu_ln_g, sgu_ln_b, w_s, b_s, w_out, ln1_g, ln1_b, w_mq, w_mk, w_mv, w_mo, ln2_g, ln2_b, w_up, w_down, ln3_g, ln3_b):
    depth = w_in.shape[0]
    assert depth == 1
    b, seq, d = x_prompt.shape
    bs_, t_new, _ = x_sample.shape
    past, n_heads, dh = cache_fox_k.shape[2:]
    n_mem, mem_heads, mem_dh = cache_mem_k.shape[2:]
    n_groups, chunk = w_s.shape[1], w_s.shape[2]
    fox_w = n_heads * dh
    gw = d - fox_w
    alpha = (2 * depth) ** 0.25
    l = 0

    wi = w_in[l]
    o3 = 3 * fox_w
    o4 = o3 + n_heads
    wqkv = wi[:, :o3].astype(BF16)
    wf = _pad_lanes(wi[:, o3:o4]).astype(BF16)
    bf = _pad_lanes(b_f[l][None, :])
    wug = wi[:, o4:].astype(BF16)
    lg, lb = sgu_ln_g[l][None, :], sgu_ln_b[l][None, :]
    ws_p, bs_p = w_s[l], b_s[l][:, :, None]
    reps = chunk // t_new
    ws_s = jnp.tile(w_s[l][:, :t_new, :t_new], (1, reps, reps))
    bs_s = jnp.tile(b_s[l][:, :t_new], (1, reps))[:, :, None]
    wo = w_out[l].astype(BF16)
    wmq, wmk, wmv, wmo = (w[l].astype(BF16) for w in (w_mq, w_mk, w_mv, w_mo))
    wu, wd = w_up[l].astype(BF16), w_down[l].astype(BF16)
    ln = [a[l][None, :] for a in (ln1_g, ln1_b, ln2_g, ln2_b, ln3_g, ln3_b)]

    def post(h, fo, go, mk, mv, *, rows_per_batch, tm_attn):
        tm_out = min(OUT_PROJ_TILE, h.shape[0])
        h = _out_proj(fo, go, h, wo, ln[0], ln[1], tm=tm_out, alpha=alpha, splits=tm_out // ROW_GROUP)
        h = _mem_attn(h, wmq, wmo, mk, mv, ln[2], ln[3], tm=tm_attn, rows_per_batch=rows_per_batch,
                      n_heads=mem_heads, n_mem=n_mem, alpha=alpha, scale=mem_dh ** -0.5,
                      splits=ROW_TILE // ROW_GROUP)
        return _ffn(h, wu, wd, ln[4], ln[5], tm=ROW_TILE, tf=FFN_CHUNK, alpha=alpha,
                    groups=ROW_TILE // ROW_GROUP)

    xp = x_prompt.reshape(b * seq, d)
    q, k, v, logf, go, kb, vb = _in_proj(xp, wqkv, wf, bf, wug, lg, lb, ws_p, bs_p, tm=ROW_TILE, period=chunk,
                                         n_heads=n_heads, q_scale=dh ** -0.5 * LOG2E,
                                         emit_kv_bf16=True, emit_g=False)
    lt = jnp.transpose(logf.reshape(b, seq, n_heads), (0, 2, 1)).reshape(b * n_heads, seq // LANES, LANES)
    c = _cumsum_rows(lt, per_step=n_heads)
    fo = _fox_prompt(q.reshape(b, seq, fox_w), kb.reshape(b, seq, fox_w), vb.reshape(b, seq, fox_w), c,
                     n_heads=n_heads, hpb=FOX_HEADS_PER_STEP)
    mem2 = mem_prompt.reshape(b * n_mem, d)
    mkt, mvt, mkb, mvb = _mem_proj(mem2, wmk, wmv, tm=ROW_TILE, n_heads=mem_heads)

    def untile_rows(a):
        a = a.reshape(b, n_mem, mem_dh // LANES, mem_heads, LANES)
        return jnp.transpose(a, (0, 1, 3, 2, 4)).reshape(1, b, n_mem, mem_heads, mem_dh)

    mk, mv = untile_rows(mkt), untile_rows(mvt)
    yp = post(xp, fo.reshape(b * seq, fox_w), go, mkb.reshape(b, n_mem, d), mvb.reshape(b, n_mem, d),
              rows_per_batch=seq, tm_attn=ROW_TILE)

    xs = x_sample.reshape(bs_ * t_new, d)
    qs, ks, vs, logfs, gos, gs = _in_proj(xs, wqkv, wf, bf, wug, lg, lb, ws_s, bs_s, tm=bs_ * t_new,
                                          period=t_new, n_heads=n_heads, q_scale=dh ** -0.5 * LOG2E,
                                          emit_kv_bf16=False, emit_g=True)
    rows_c = -(-(past + t_new) // LANES)
    rows_c = -(-rows_c // 8) * 8
    lcat = jnp.concatenate([cache_fox_logf[l], logfs.reshape(bs_, t_new, n_heads)], axis=1)
    lcat = jnp.pad(lcat, ((0, 0), (0, rows_c * LANES - past - t_new), (0, 0)))
    lts = jnp.transpose(lcat, (0, 2, 1)).reshape(bs_ * n_heads, rows_c, LANES)
    cs = _cumsum_rows(lts, per_step=n_heads).reshape(bs_, n_heads, rows_c * LANES)
    fos = _fox_sample(qs.reshape(bs_, t_new, fox_w), ks, vs,
                      cache_fox_k.reshape(bs_ * past * n_heads, dh), cache_fox_v.reshape(bs_ * past * n_heads, dh),
                      cs, tk=FOX_SAMPLE_KEYS)
    def tile_rows(a):
        a = a.reshape(bs_, n_mem, mem_heads, mem_dh // LANES, LANES)
        return jnp.transpose(a, (0, 1, 3, 2, 4)).reshape(bs_ * n_mem * d // LANES, LANES)

    ys = post(xs, fos.reshape(bs_ * t_new, fox_w), gos, tile_rows(cache_mem_k), tile_rows(cache_mem_v),
              rows_per_batch=t_new, tm_attn=MEM_ATTN_SAMPLE_ROWS)

    return (yp.reshape(b, seq, d), ys.reshape(bs_, t_new, d),
            k.reshape(1, b, seq, n_heads, dh), v.reshape(1, b, seq, n_heads, dh),
            logf.reshape(1, b, seq, n_heads),
            mk, mv,
            ks.reshape(1, bs_, t_new, n_heads, dh), vs.reshape(1, bs_, t_new, n_heads, dh),
            logfs.reshape(1, bs_, t_new, n_heads), gs.reshape(1, bs_, t_new, gw))
```

```python
import functools
import math

import jax
import jax.numpy as jnp
from jax import lax
from jax.experimental import pallas as pl
from jax.experimental.pallas import tpu as pltpu

F32 = jnp.float32
BF16 = jnp.bfloat16

LN_EPS = 1e-5
LANES = 128
BF16_ROWS = 16
GELU_C = math.sqrt(2.0 / math.pi)
LOG2E = 1.0 / math.log(2.0)
VMEM_LIMIT = 56 * 1024 * 1024

ROW_TILE = 512
ROW_GROUP = 256
OUT_PROJ_TILE = 1024
FFN_CHUNK = 1024
FFN_SLAB = 512
FOX_KEY_BLOCK = 2 * LANES
FOX_HEADS_PER_STEP = 4
FOX_PREP_ROWS = 512
FOX_SAMPLE_KEYS = 2048
MEM_ATTN_SAMPLE_ROWS = 128


def _dot(a, b):
    return jnp.dot(a, b, preferred_element_type=F32)


def _dot_nt(a, b):
    return lax.dot_general(a, b, (((1,), (1,)), ((), ())), preferred_element_type=F32)


def _gelu(x):
    return 0.5 * x * (1.0 + jnp.tanh(GELU_C * (x + 0.044715 * (x * x * x))))


def _layer_norm(x, g, b):
    mu = jnp.mean(x, axis=-1, keepdims=True)
    xc = x - mu
    var = jnp.mean(xc * xc, axis=-1, keepdims=True)
    return xc * lax.rsqrt(var + LN_EPS) * g + b


def _log_sigmoid(x):
    return jnp.minimum(x, 0.0) - jnp.log1p(jnp.exp(-jnp.abs(x)))


def _resident(shape):
    nd = len(shape)
    return pl.BlockSpec(shape, lambda *_: (0,) * nd, pipeline_mode=pl.Buffered(1))


def _params(*sem, vmem=VMEM_LIMIT):
    return pltpu.CompilerParams(dimension_semantics=sem, vmem_limit_bytes=vmem)


def _in_proj_kernel(x_ref, wqkv_ref, wf_ref, bf_ref, wug_ref, lg_ref, lb_ref, ws_ref, bs_ref,
                    q_ref, k_ref, v_ref, logf_ref, go_ref, *opt_refs,
                    fox_w, n_heads, gw, n_groups, period, q_scale, emit_kv_bf16, emit_g):
    tm = x_ref.shape[0]
    chunk = ws_ref.shape[1]
    gd = gw // n_groups
    dh = fox_w // n_heads
    opt = list(opt_refs)
    kv_bf16_refs = (opt.pop(0), opt.pop(0)) if emit_kv_bf16 else (None, None)
    xb = x_ref[...].astype(BF16)
    q_ref[...] = (_dot(xb, wqkv_ref[:, 0:fox_w]) * q_scale).astype(BF16)
    for idx, (rows_ref, copy_ref) in enumerate(zip((k_ref, v_ref), kv_bf16_refs)):
        y = _dot(xb, wqkv_ref[:, (idx + 1) * fox_w:(idx + 2) * fox_w])
        for h in range(n_heads):
            rows_ref[pl.ds(h, tm, stride=n_heads), :] = y[:, h * dh:(h + 1) * dh]
        if copy_ref is not None:
            copy_ref[...] = y.astype(BF16)
    zf = _dot(xb, wf_ref[...]) + bf_ref[...]
    logf_ref[...] = _log_sigmoid(zf).T[0:n_heads, :]
    g = _layer_norm(_gelu(_dot(xb, wug_ref[:, gw:2 * gw])), lg_ref[...], lb_ref[...])
    if emit_g:
        opt.pop(0)[...] = g
    gb = g.astype(BF16)
    u = _gelu(_dot(xb, wug_ref[:, 0:gw]))
    r = lax.broadcasted_iota(jnp.int32, (chunk, chunk), 0)
    c = lax.broadcasted_iota(jnp.int32, (chunk, chunk), 1)
    sh = period.bit_length() - 1
    keep = ((r >> sh) == (c >> sh)) & ((c & (period - 1)) <= (r & (period - 1)))
    for gi in range(n_groups):
        wsg = jnp.where(keep, ws_ref[gi], 0.0).astype(BF16)
        cols = slice(gi * gd, (gi + 1) * gd)
        for ci in range(tm // chunk):
            rows = slice(ci * chunk, (ci + 1) * chunk)
            s = _dot(wsg, gb[rows, cols]) + bs_ref[gi]
            go_ref[rows, cols] = (u[rows, cols] * s).astype(BF16)


def _in_proj(x, wqkv, wf, bf, wug, lg, lb, ws, bs, *, tm, period, n_heads, q_scale, emit_kv_bf16, emit_g):
    m, d = x.shape
    fox_w = wqkv.shape[1] // 3
    gw = wug.shape[1] // 2
    n_groups = ws.shape[0]
    dh = fox_w // n_heads
    assert dh == LANES and period & (period - 1) == 0 and ws.shape[1] % period == 0
    row = lambda w: pl.BlockSpec((tm, w), lambda i: (i, 0))
    head_rows = pl.BlockSpec((tm * n_heads, dh), lambda i: (i, 0))
    out_shape = [jax.ShapeDtypeStruct((m, fox_w), BF16),
                 jax.ShapeDtypeStruct((m * n_heads, dh), F32),
                 jax.ShapeDtypeStruct((m * n_heads, dh), F32),
                 jax.ShapeDtypeStruct((n_heads, m), F32),
                 jax.ShapeDtypeStruct((m, gw), BF16)]
    out_specs = [row(fox_w), head_rows, head_rows, pl.BlockSpec((n_heads, tm), lambda i: (0, i)), row(gw)]
    if emit_kv_bf16:
        out_shape += [jax.ShapeDtypeStruct((m, fox_w), BF16)] * 2
        out_specs += [row(fox_w)] * 2
    if emit_g:
        out_shape.append(jax.ShapeDtypeStruct((m, gw), F32))
        out_specs.append(row(gw))
    kern = functools.partial(_in_proj_kernel, fox_w=fox_w, n_heads=n_heads, gw=gw, n_groups=n_groups,
                             period=period, q_scale=q_scale, emit_kv_bf16=emit_kv_bf16, emit_g=emit_g)
    return pl.pallas_call(
        kern, grid=(m // tm,),
        in_specs=[row(d), _resident(wqkv.shape), _resident(wf.shape), _resident(bf.shape),
                  _resident(wug.shape), _resident(lg.shape), _resident(lb.shape),
                  _resident(ws.shape), _resident(bs.shape)],
        out_specs=out_specs, out_shape=out_shape,
        compiler_params=_params("parallel"), name="in_proj_sgu",
    )(x, wqkv, wf, bf, wug, lg, lb, ws, bs)


def _cumsum_kernel(x_ref, o_ref, *, rows):
    x = x_ref[...]
    n = x.shape[0]
    hi = lax.Precision.HIGHEST
    ii = lax.broadcasted_iota(jnp.int32, (LANES, LANES), 0)
    jj = lax.broadcasted_iota(jnp.int32, (LANES, LANES), 1)
    within = jnp.dot(x, (ii <= jj).astype(F32), precision=hi, preferred_element_type=F32)
    totals = jnp.dot(x, jnp.ones((LANES, LANES), F32), precision=hi, preferred_element_type=F32)
    ri = lax.broadcasted_iota(jnp.int32, (n, n), 0)
    rj = lax.broadcasted_iota(jnp.int32, (n, n), 1)
    seq_i = jnp.zeros((n, n), jnp.int32)
    seq_j = jnp.zeros((n, n), jnp.int32)
    for s in range(rows, n, rows):
        seq_i += (ri >= s).astype(jnp.int32)
        seq_j += (rj >= s).astype(jnp.int32)
    earlier = ((seq_i == seq_j) & (rj < ri)).astype(F32)
    o_ref[...] = within + jnp.dot(earlier, totals, precision=hi, preferred_element_type=F32)


def _cumsum_rows(x, *, per_step):
    n, rows, _ = x.shape
    x2 = x.reshape(n * rows, LANES)
    spec = pl.BlockSpec((per_step * rows, LANES), lambda i: (i, 0))
    out = pl.pallas_call(
        functools.partial(_cumsum_kernel, rows=rows), grid=(n // per_step,),
        in_specs=[spec], out_specs=spec,
        out_shape=jax.ShapeDtypeStruct(x2.shape, F32),
        compiler_params=_params("parallel"), name="logf_cumsum",
    )(x2)
    return out.reshape(n, rows * LANES)


def _fox_prompt_kernel(q_ref, k_ref, v_ref, c_ref, mask_ref, o_ref, ka_ref, vt_ref, qa_ref, sa_ref, sb_ref,
                       pa_ref, pb_ref, aa_ref, ab_ref, m_ref, acc_ref, *, hk, hpb):
    qi = pl.program_id(2)
    tq = q_ref.shape[1]
    seq = k_ref.shape[1]
    dh = k_ref.shape[2] // hpb
    prep = FOX_PREP_ROWS
    ones_rows = (lax.broadcasted_iota(jnp.int32, (BF16_ROWS, prep), 0) == 0).astype(BF16)

    @pl.when(qi == 0)
    def _():
        row = lax.broadcasted_iota(jnp.int32, (dh, prep), 0)
        for hh in range(hpb):
            hc = slice(hh * dh, (hh + 1) * dh)
            for r0 in range(0, seq, prep):
                rs = slice(r0, r0 + prep)
                ka_ref[hh, rs, 0:dh] = k_ref[0, rs, hc]
                vt_ref[hh, 0:dh, rs] = v_ref[0, rs, hc].astype(F32).T.astype(BF16)
                vt_ref[hh, dh:dh + BF16_ROWS, rs] = ones_rows
                c2 = c_ref[0, hh:hh + 1, rs] * LOG2E
                hi = c2.astype(BF16).astype(F32)
                r1 = c2 - hi
                mid = r1.astype(BF16).astype(F32)
                lo = r1 - mid
                aug = jnp.where(row == 0, -hi, jnp.where(row == 1, -mid, jnp.where(row == 2, -lo, 0.0)))
                ka_ref[hh, rs, dh:2 * dh] = aug.T.astype(BF16)

    ones = (lax.broadcasted_iota(jnp.int32, (dh, tq), 0) < 3).astype(BF16)
    for hh in range(hpb):
        qa_ref[hh, 0:dh, :] = q_ref[0, :, hh * dh:(hh + 1) * dh].astype(F32).T.astype(BF16)
        qa_ref[hh, dh:2 * dh, :] = ones

    def qk(hh, blk):
        start = pl.multiple_of(blk * hk, hk)
        return _dot(ka_ref[hh, pl.ds(start, hk), :], qa_ref[hh])

    def softmax(hh, s, p_ref, a_ref):
        m = m_ref[hh]
        m_new = jnp.maximum(m, jnp.max(s, axis=0, keepdims=True))
        p = jnp.exp2(s - m_new)
        a = jnp.exp2(m - m_new)
        m_ref[hh] = m_new
        p_ref[hh] = p.astype(BF16)
        a_ref[hh] = a

    def pv(hh, blk, p_ref, a_ref):
        start = pl.multiple_of(blk * hk, hk)
        acc_ref[hh] = a_ref[hh] * acc_ref[hh] + _dot(vt_ref[hh, :, pl.ds(start, hk)], p_ref[hh])

    for hh in range(hpb):
        m_ref[hh] = jnp.full((1, tq), -jnp.inf, F32)
        acc_ref[hh] = jnp.zeros((dh + BF16_ROWS, tq), F32)
        pb_ref[hh] = jnp.zeros((hk, tq), BF16)
        ab_ref[hh] = jnp.ones((1, tq), F32)
        sa_ref[hh] = qk(hh, 0)

    heads = range(hpb)

    def body(j, carry):
        for hh in heads:
            sb_ref[hh] = qk(hh, 2 * j + 1)
            pv(hh, jnp.maximum(2 * j - 1, 0), pb_ref, ab_ref)
            softmax(hh, sa_ref[hh], pa_ref, aa_ref)
            sa_ref[hh] = qk(hh, 2 * j + 2)
            pv(hh, 2 * j, pa_ref, aa_ref)
            softmax(hh, sb_ref[hh], pb_ref, ab_ref)
        return carry

    lax.fori_loop(0, qi, body, 0)
    for hh in heads:
        sb_ref[hh] = qk(hh, 2 * qi + 1)
    for hh in heads:
        pv(hh, jnp.maximum(2 * qi - 1, 0), pb_ref, ab_ref)
    for hh in heads:
        softmax(hh, sa_ref[hh] + mask_ref[0], pa_ref, aa_ref)
    for hh in heads:
        pv(hh, 2 * qi, pa_ref, aa_ref)
    for hh in heads:
        softmax(hh, sb_ref[hh] + mask_ref[1], pb_ref, ab_ref)
    for hh in heads:
        pv(hh, 2 * qi + 1, pb_ref, ab_ref)
    for hh in heads:
        out = acc_ref[hh, 0:dh, :] / acc_ref[hh, dh:dh + 1, :]
        o_ref[0, :, hh * dh:(hh + 1) * dh] = out.T.astype(o_ref.dtype)


def _fox_prompt(q, k, v, c, *, n_heads, hpb):
    b, seq, width = q.shape
    dh = width // n_heads
    hk = FOX_KEY_BLOCK
    tq = 2 * hk
    groups = n_heads // hpb
    cg = c.reshape(b * groups, hpb, seq)
    wide = hpb * dh
    key = jnp.arange(2 * hk, dtype=jnp.int32).reshape(2, hk, 1)
    mask = jnp.where(key <= jnp.arange(tq, dtype=jnp.int32)[None, None, :], 0.0, -jnp.inf).astype(F32)
    return pl.pallas_call(
        functools.partial(_fox_prompt_kernel, hk=hk, hpb=hpb),
        grid=(b, groups, seq // tq),
        in_specs=[pl.BlockSpec((1, tq, wide), lambda bi, h, i: (bi, i, h)),
                  pl.BlockSpec((1, seq, wide), lambda bi, h, i: (bi, 0, h)),
                  pl.BlockSpec((1, seq, wide), lambda bi, h, i: (bi, 0, h)),
                  pl.BlockSpec((1, hpb, seq), lambda bi, h, i: (bi * groups + h, 0, 0)),
                  _resident(mask.shape)],
        out_specs=pl.BlockSpec((1, tq, wide), lambda bi, h, i: (bi, i, h)),
        out_shape=jax.ShapeDtypeStruct((b, seq, width), BF16),
        scratch_shapes=[pltpu.VMEM((hpb, seq, 2 * dh), BF16), pltpu.VMEM((hpb, dh + BF16_ROWS, seq), BF16),
                        pltpu.VMEM((hpb, 2 * dh, tq), BF16),
                        pltpu.VMEM((hpb, hk, tq), F32), pltpu.VMEM((hpb, hk, tq), F32),
                        pltpu.VMEM((hpb, hk, tq), BF16), pltpu.VMEM((hpb, hk, tq), BF16),
                        pltpu.VMEM((hpb, 1, tq), F32), pltpu.VMEM((hpb, 1, tq), F32),
                        pltpu.VMEM((hpb, 1, tq), F32),
                        pltpu.VMEM((hpb, dh + BF16_ROWS, tq), F32)],
        compiler_params=_params("parallel", "parallel", "arbitrary"), name="fox_prompt",
    )(q, k, v, cg, mask)


def _fox_sample_kernel(q_ref, kn_ref, vn_ref, kc_ref, vc_ref, c_ref, o_ref, m_ref, l_ref, acc_ref, *, n_heads):
    ci = pl.program_id(1)
    dh = kc_ref.shape[1]
    tk = kc_ref.shape[0] // n_heads
    t_new = q_ref.shape[1]
    past = pl.num_programs(1) * tk

    def update(scores, values):
        s = jnp.concatenate(scores, axis=0)
        m = m_ref[...]
        m_new = jnp.maximum(m, jnp.max(s, axis=-1, keepdims=True))
        p = jnp.exp2(s - m_new)
        a = jnp.exp2(m - m_new)
        m_ref[...] = m_new
        l_ref[...] = a * l_ref[...] + jnp.sum(p, axis=-1, keepdims=True)
        pb = p.astype(BF16)
        pv = [_dot(pb[h * t_new:(h + 1) * t_new, :], values[h]) for h in range(n_heads)]
        acc_ref[...] = a * acc_ref[...] + jnp.concatenate(pv, axis=0)

    heads = [slice(h * dh, (h + 1) * dh) for h in range(n_heads)]

    @pl.when(ci == 0)
    def _():
        m_ref[...] = jnp.full(m_ref.shape, -jnp.inf, F32)
        l_ref[...] = jnp.zeros(l_ref.shape, F32)
        acc_ref[...] = jnp.zeros(acc_ref.shape, F32)
        r = lax.broadcasted_iota(jnp.int32, (t_new, t_new), 0)
        c = lax.broadcasted_iota(jnp.int32, (t_new, t_new), 1)
        scores = []
        for h, hc in enumerate(heads):
            s = _dot_nt(q_ref[0, :, hc], kn_ref[pl.ds(h, t_new, stride=n_heads), :].astype(BF16))
            s = s - c_ref[0, h:h + 1, past:past + t_new] * LOG2E
            scores.append(jnp.where(c <= r, s, -jnp.inf))
        update(scores, [vn_ref[pl.ds(h, t_new, stride=n_heads), :].astype(BF16) for h in range(n_heads)])

    start = pl.multiple_of(ci * tk, tk)
    scores = []
    for h, hc in enumerate(heads):
        s = _dot_nt(q_ref[0, :, hc], kc_ref[pl.ds(h, tk, stride=n_heads), :].astype(BF16))
        scores.append(s - c_ref[0, h:h + 1, pl.ds(start, tk)] * LOG2E)
    update(scores, [vc_ref[pl.ds(h, tk, stride=n_heads), :].astype(BF16) for h in range(n_heads)])

    @pl.when(ci == pl.num_programs(1) - 1)
    def _():
        out = acc_ref[...] / l_ref[...]
        for h, hc in enumerate(heads):
            o_ref[0, :, hc] = out[h * t_new:(h + 1) * t_new, :].astype(o_ref.dtype)


def _fox_sample(q, kn, vn, kc, vc, c, *, tk):
    b, t_new, width = q.shape
    dh = kc.shape[1]
    n_heads = width // dh
    past = kc.shape[0] // (b * n_heads)
    chunks = past // tk
    new = pl.BlockSpec((1, t_new, width), lambda bi, ci: (bi, 0, 0))
    new_rows = pl.BlockSpec((t_new * n_heads, dh), lambda bi, ci: (bi, 0))
    old = pl.BlockSpec((tk * n_heads, dh), lambda bi, ci: (bi * chunks + ci, 0))
    return pl.pallas_call(
        functools.partial(_fox_sample_kernel, n_heads=n_heads), grid=(b, chunks),
        in_specs=[new, new_rows, new_rows, old, old,
                  pl.BlockSpec((1, n_heads, c.shape[2]), lambda bi, ci: (bi, 0, 0))],
        out_specs=new,
        out_shape=jax.ShapeDtypeStruct((b, t_new, width), BF16),
        scratch_shapes=[pltpu.VMEM((n_heads * t_new, 1), F32), pltpu.VMEM((n_heads * t_new, 1), F32),
                        pltpu.VMEM((n_heads * t_new, dh), F32)],
        compiler_params=_params("parallel", "arbitrary"), name="fox_sample",
    )(q, kn, vn, kc, vc, c)


def _out_proj_kernel(fo_ref, go_ref, x_ref, w_ref, g_ref, b_ref, o_ref, *, alpha, splits):
    half = fo_ref.shape[1]
    sub = x_ref.shape[0] // splits
    for r0 in range(0, x_ref.shape[0], sub):
        rows = slice(r0, r0 + sub)
        mix = _dot(fo_ref[rows, :], w_ref[0:half, :]) + _dot(go_ref[rows, :], w_ref[half:2 * half, :])
        o_ref[rows, :] = _layer_norm(alpha * x_ref[rows, :] + mix, g_ref[...], b_ref[...])


def _out_proj(fo, go, x, w, g, b, *, tm, alpha, splits):
    m, d = x.shape
    row = lambda wd: pl.BlockSpec((tm, wd), lambda i: (i, 0))
    return pl.pallas_call(
        functools.partial(_out_proj_kernel, alpha=alpha, splits=splits), grid=(m // tm,),
        in_specs=[row(fo.shape[1]), row(go.shape[1]), row(d),
                  _resident(w.shape), _resident(g.shape), _resident(b.shape)],
        out_specs=row(d), out_shape=jax.ShapeDtypeStruct((m, d), F32),
        compiler_params=_params("parallel"), name="out_proj_ln",
    )(fo, go, x, w, g, b)


def _mem_proj_kernel(x_ref, wk_ref, wv_ref, kt_ref, vt_ref, kb_ref, vb_ref, *, n_heads):
    tm, d = x_ref.shape
    lane_tiles = d // n_heads // LANES
    stride = lane_tiles * n_heads
    xb = x_ref[...].astype(BF16)
    for w_ref, t_ref, b_ref in ((wk_ref, kt_ref, kb_ref), (wv_ref, vt_ref, vb_ref)):
        y = _dot(xb, w_ref[...])
        b_ref[...] = y.astype(BF16)
        for hd in range(n_heads):
            for j in range(lane_tiles):
                c0 = (hd * lane_tiles + j) * LANES
                t_ref[pl.ds(j * n_heads + hd, tm, stride=stride), :] = y[:, c0:c0 + LANES]


def _mem_proj(x, wk, wv, *, tm, n_heads):
    m, d = x.shape
    rows = pl.BlockSpec((tm, d), lambda i: (i, 0))
    tiled = pl.BlockSpec((tm * d // LANES, LANES), lambda i: (i, 0))
    return pl.pallas_call(
        functools.partial(_mem_proj_kernel, n_heads=n_heads), grid=(m // tm,),
        in_specs=[rows, _resident(wk.shape), _resident(wv.shape)],
        out_specs=[tiled, tiled, rows, rows],
        out_shape=[jax.ShapeDtypeStruct((m * d // LANES, LANES), F32)] * 2
        + [jax.ShapeDtypeStruct((m, d), BF16)] * 2,
        compiler_params=_params("parallel"), name="mem_proj",
    )(x, wk, wv)


def _mem_attn_kernel(h_ref, wq_ref, wo_ref, mk_ref, mv_ref, g_ref, b_ref, o_ref, att_ref,
                     *, n_heads, n_mem, nb, alpha, scale, splits):
    tm, d = h_ref.shape
    dh = d // n_heads
    lane_tiles = dh // LANES

    def mem_head(ref, bi, hd):
        if len(ref.shape) == 3:
            return ref[bi, :, hd * dh:(hd + 1) * dh].astype(BF16)
        stride = lane_tiles * n_heads
        parts = [ref[pl.ds(bi * n_mem * stride + j * n_heads + hd, n_mem, stride=stride), :]
                 for j in range(lane_tiles)]
        return jnp.concatenate(parts, axis=1).astype(BF16)

    groups = splits if nb == 1 else 1
    rg = tm // groups
    rb = rg // nb
    pairs = [(bi, hd) for bi in range(nb) for hd in range(n_heads)]
    group_rows = [slice(gi * rg, (gi + 1) * rg) for gi in range(groups)]
    qbs = [(_dot(h_ref[rows, :].astype(BF16), wq_ref[...]) * scale).astype(BF16) for rows in group_rows]
    for gi, qb in enumerate(qbs):
        s = jnp.concatenate([_dot_nt(qb[bi * rb:(bi + 1) * rb, hd * dh:(hd + 1) * dh], mem_head(mk_ref, bi, hd))
                             for bi, hd in pairs], axis=0)
        p = jnp.exp(s - jnp.max(s, axis=-1, keepdims=True))
        l = jnp.sum(p, axis=-1, keepdims=True)
        pb = p.astype(BF16)
        for k, (bi, hd) in enumerate(pairs):
            o = _dot(pb[k * rb:(k + 1) * rb, :], mem_head(mv_ref, bi, hd)) / l[k * rb:(k + 1) * rb, :]
            att_ref[gi * rg + bi * rb:gi * rg + (bi + 1) * rb, hd * dh:(hd + 1) * dh] = o.astype(BF16)
    ys = [_dot(att_ref[rows, :], wo_ref[...]) for rows in group_rows]
    for rows, y in zip(group_rows, ys):
        o_ref[rows, :] = _layer_norm(alpha * h_ref[rows, :] + y, g_ref[...], b_ref[...])


def _mem_attn(h, wq, wo, mk, mv, g, b, *, tm, rows_per_batch, n_heads, n_mem, alpha, scale, splits):
    m, d = h.shape
    if tm <= rows_per_batch:
        nb = 1
        steps_per_batch = rows_per_batch // tm
        batch_of = lambda i: i // steps_per_batch
    else:
        nb = tm // rows_per_batch
        batch_of = lambda i: i
    row = pl.BlockSpec((tm, d), lambda i: (i, 0))
    if mk.ndim == 3:
        mem = pl.BlockSpec((nb, n_mem, d), lambda i: (batch_of(i), 0, 0))
    else:
        mem = pl.BlockSpec((nb * n_mem * d // LANES, LANES), lambda i: (batch_of(i), 0))
    return pl.pallas_call(
        functools.partial(_mem_attn_kernel, n_heads=n_heads, n_mem=n_mem, nb=nb, alpha=alpha, scale=scale,
                          splits=splits),
        grid=(m // tm,),
        in_specs=[row, _resident(wq.shape), _resident(wo.shape), mem, mem,
                  _resident(g.shape), _resident(b.shape)],
        out_specs=row, out_shape=jax.ShapeDtypeStruct((m, d), F32),
        scratch_shapes=[pltpu.VMEM((tm, d), BF16)],
        compiler_params=_params("parallel"), name="mem_attn_ln",
    )(h, wq, wo, mk, mv, g, b)


def _ffn_kernel(x_ref, wu_ref, wd_ref, g_ref, b_ref, o_ref, xb_ref, *, alpha, groups):
    f = pl.program_id(1)
    last = pl.num_programs(1) - 1
    tm, d = o_ref.shape
    slab = FFN_SLAB

    def hidden(xb):
        a = jnp.maximum(_dot(xb, wu_ref[...]), 0.0)
        return (a * a).astype(BF16)

    @pl.when(f == 0)
    def _():
        xb = x_ref[...].astype(BF16)
        xb_ref[...] = xb
        ab = hidden(xb)
        for n0 in range(0, d, slab):
            o_ref[:, n0:n0 + slab] = _dot(ab, wd_ref[:, n0:n0 + slab])

    @pl.when((f > 0) & (f < last))
    def _():
        ab = hidden(xb_ref[...])
        for n0 in range(0, d, slab):
            o_ref[:, n0:n0 + slab] += _dot(ab, wd_ref[:, n0:n0 + slab])

    @pl.when(f == last)
    def _():
        rg = tm // groups
        for r0 in range(0, tm, rg):
            rows = slice(r0, r0 + rg)
            y = o_ref[rows, :] + _dot(hidden(xb_ref[rows, :]), wd_ref[...])
            o_ref[rows, :] = _layer_norm(alpha * x_ref[rows, :] + y, g_ref[...], b_ref[...])


def _ffn(x, wu, wd, g, b, *, tm, tf, alpha, groups):
    m, d = x.shape
    dff = wu.shape[1]
    assert dff // tf >= 2
    row = pl.BlockSpec((tm, d), lambda i, f: (i, 0))
    return pl.pallas_call(
        functools.partial(_ffn_kernel, alpha=alpha, groups=groups), grid=(m // tm, dff // tf),
        in_specs=[row, pl.BlockSpec((d, tf), lambda i, f: (0, f)),
                  pl.BlockSpec((tf, d), lambda i, f: (f, 0)),
                  _resident(g.shape), _resident(b.shape)],
        out_specs=row, out_shape=jax.ShapeDtypeStruct((m, d), F32),
        scratch_shapes=[pltpu.VMEM((tm, d), BF16)],
        compiler_params=_params("parallel", "arbitrary"), name="ffn_ln",
    )(x, wu, wd, g, b)


def _pad_lanes(a):
    return jnp.pad(a, ((0, 0), (0, LANES - a.shape[1])))


def kernel(x_prompt, x_sample, mem_prompt, cache_fox_k, cache_fox_v, cache_fox_logf, cache_mem_k, cache_mem_v, w_in, b_f, sgu_ln_g, sgu_ln_b, w_s, b_s, w_out, ln1_g, ln1_b, w_mq, w_mk, w_mv, w_mo, ln2_g, ln2_b, w_up, w_down, ln3_g, ln3_b):
    depth = w_in.shape[0]
    assert depth == 1
    b, seq, d = x_prompt.shape
    bs_, t_new, _ = x_sample.shape
    past, n_heads, dh = cache_fox_k.shape[2:]
    n_mem, mem_heads, mem_dh = cache_mem_k.shape[2:]
    n_groups, chunk = w_s.shape[1], w_s.shape[2]
    fox_w = n_heads * dh
    gw = d - fox_w
    alpha = (2 * depth) ** 0.25
    l = 0

    wi = w_in[l]
    o3 = 3 * fox_w
    o4 = o3 + n_heads
    wqkv = wi[:, :o3].astype(BF16)
    wf = _pad_lanes(wi[:, o3:o4]).astype(BF16)
    bf = _pad_lanes(b_f[l][None, :])
    wug = wi[:, o4:].astype(BF16)
    lg, lb = sgu_ln_g[l][None, :], sgu_ln_b[l][None, :]
    ws_p, bs_p = w_s[l], b_s[l][:, :, None]
    reps = chunk // t_new
    ws_s = jnp.tile(w_s[l][:, :t_new, :t_new], (1, reps, reps))
    bs_s = jnp.tile(b_s[l][:, :t_new], (1, reps))[:, :, None]
    wo = w_out[l].astype(BF16)
    wmq, wmk, wmv, wmo = (w[l].astype(BF16) for w in (w_mq, w_mk, w_mv, w_mo))
    wu, wd = w_up[l].astype(BF16), w_down[l].astype(BF16)
    ln = [a[l][None, :] for a in (ln1_g, ln1_b, ln2_g, ln2_b, ln3_g, ln3_b)]

    def post(h, fo, go, mk, mv, *, rows_per_batch, tm_attn):
        tm_out = min(OUT_PROJ_TILE, h.shape[0])
        h = _out_proj(fo, go, h, wo, ln[0], ln[1], tm=tm_out, alpha=alpha, splits=tm_out // ROW_GROUP)
        h = _mem_attn(h, wmq, wmo, mk, mv, ln[2], ln[3], tm=tm_attn, rows_per_batch=rows_per_batch,
                      n_heads=mem_heads, n_mem=n_mem, alpha=alpha, scale=mem_dh ** -0.5,
                      splits=ROW_TILE // ROW_GROUP)
        return _ffn(h, wu, wd, ln[4], ln[5], tm=ROW_TILE, tf=FFN_CHUNK, alpha=alpha,
                    groups=ROW_TILE // ROW_GROUP)

    xp = x_prompt.reshape(b * seq, d)
    q, k, v, logf, go, kb, vb = _in_proj(xp, wqkv, wf, bf, wug, lg, lb, ws_p, bs_p, tm=ROW_TILE, period=chunk,
                                         n_heads=n_heads, q_scale=dh ** -0.5 * LOG2E,
                                         emit_kv_bf16=True, emit_g=False)
    logf = logf.reshape(n_heads, b, seq)
    lt = jnp.transpose(logf, (1, 0, 2)).reshape(b * n_heads, seq // LANES, LANES)
    c = _cumsum_rows(lt, per_step=n_heads)
    fo = _fox_prompt(q.reshape(b, seq, fox_w), kb.reshape(b, seq, fox_w), vb.reshape(b, seq, fox_w), c,
                     n_heads=n_heads, hpb=FOX_HEADS_PER_STEP)
    mem2 = mem_prompt.reshape(b * n_mem, d)
    mkt, mvt, mkb, mvb = _mem_proj(mem2, wmk, wmv, tm=ROW_TILE, n_heads=mem_heads)

    def untile_rows(a):
        a = a.reshape(b, n_mem, mem_dh // LANES, mem_heads, LANES)
        return jnp.transpose(a, (0, 1, 3, 2, 4)).reshape(1, b, n_mem, mem_heads, mem_dh)

    mk, mv = untile_rows(mkt), untile_rows(mvt)
    yp = post(xp, fo.reshape(b * seq, fox_w), go, mkb.reshape(b, n_mem, d), mvb.reshape(b, n_mem, d),
              rows_per_batch=seq, tm_attn=ROW_TILE)

    xs = x_sample.reshape(bs_ * t_new, d)
    qs, ks, vs, logfs, gos, gs = _in_proj(xs, wqkv, wf, bf, wug, lg, lb, ws_s, bs_s, tm=bs_ * t_new,
                                          period=t_new, n_heads=n_heads, q_scale=dh ** -0.5 * LOG2E,
                                          emit_kv_bf16=False, emit_g=True)
    rows_c = -(-(past + t_new) // LANES)
    rows_c = -(-rows_c // 8) * 8
    logfs = jnp.transpose(logfs.reshape(n_heads, bs_, t_new), (1, 2, 0))
    lcat = jnp.concatenate([cache_fox_logf[l], logfs], axis=1)
    lcat = jnp.pad(lcat, ((0, 0), (0, rows_c * LANES - past - t_new), (0, 0)))
    lts = jnp.transpose(lcat, (0, 2, 1)).reshape(bs_ * n_heads, rows_c, LANES)
    cs = _cumsum_rows(lts, per_step=n_heads).reshape(bs_, n_heads, rows_c * LANES)
    fos = _fox_sample(qs.reshape(bs_, t_new, fox_w), ks, vs,
                      cache_fox_k.reshape(bs_ * past * n_heads, dh), cache_fox_v.reshape(bs_ * past * n_heads, dh),
                      cs, tk=FOX_SAMPLE_KEYS)
    def tile_rows(a):
        a = a.reshape(bs_, n_mem, mem_heads, mem_dh // LANES, LANES)
        return jnp.transpose(a, (0, 1, 3, 2, 4)).reshape(bs_ * n_mem * d // LANES, LANES)

    ys = post(xs, fos.reshape(bs_ * t_new, fox_w), gos, tile_rows(cache_mem_k), tile_rows(cache_mem_v),
              rows_per_batch=t_new, tm_attn=MEM_ATTN_SAMPLE_ROWS)

    return (yp.reshape(b, seq, d), ys.reshape(bs_, t_new, d),
            k.reshape(1, b, seq, n_heads, dh), v.reshape(1, b, seq, n_heads, dh),
            jnp.transpose(logf, (1, 2, 0)).reshape(1, b, seq, n_heads),
            mk, mv,
            ks.reshape(1, bs_, t_new, n_heads, dh), vs.reshape(1, bs_, t_new, n_heads, dh),
            logfs.reshape(1, bs_, t_new, n_heads), gs.reshape(1, bs_, t_new, gw))
```

```python
import functools
import math

import jax
import jax.numpy as jnp
from jax import lax
from jax.experimental import pallas as pl
from jax.experimental.pallas import tpu as pltpu

F32 = jnp.float32
BF16 = jnp.bfloat16

LN_EPS = 1e-5
LANES = 128
BF16_ROWS = 16
GELU_C = math.sqrt(2.0 / math.pi)
LOG2E = 1.0 / math.log(2.0)
VMEM_LIMIT = 56 * 1024 * 1024

ROW_TILE = 512
ROW_GROUP = 256
OUT_PROJ_TILE = 1024
FFN_CHUNK = 2048
VMEM_LIMIT_FFN = 60 * 1024 * 1024
FFN_SLAB = 512
FOX_KEY_BLOCK = 2 * LANES
FOX_HEADS_PER_STEP = 4
FOX_PREP_ROWS = 512
FOX_SAMPLE_KEYS = 2048
MEM_ATTN_SAMPLE_ROWS = 128


def _dot(a, b):
    return jnp.dot(a, b, preferred_element_type=F32)


def _dot_nt(a, b):
    return lax.dot_general(a, b, (((1,), (1,)), ((), ())), preferred_element_type=F32)


def _gelu(x):
    return 0.5 * x * (1.0 + jnp.tanh(GELU_C * (x + 0.044715 * (x * x * x))))


def _layer_norm(x, g, b):
    mu = jnp.mean(x, axis=-1, keepdims=True)
    xc = x - mu
    var = jnp.mean(xc * xc, axis=-1, keepdims=True)
    return xc * lax.rsqrt(var + LN_EPS) * g + b


def _log_sigmoid(x):
    return jnp.minimum(x, 0.0) - jnp.log1p(jnp.exp(-jnp.abs(x)))


def _resident(shape):
    nd = len(shape)
    return pl.BlockSpec(shape, lambda *_: (0,) * nd, pipeline_mode=pl.Buffered(1))


def _params(*sem, vmem=VMEM_LIMIT):
    return pltpu.CompilerParams(dimension_semantics=sem, vmem_limit_bytes=vmem)


def _in_proj_kernel(x_ref, wqkv_ref, wf_ref, bf_ref, wug_ref, lg_ref, lb_ref, ws_ref, bs_ref,
                    q_ref, k_ref, v_ref, logf_ref, go_ref, *opt_refs,
                    fox_w, n_heads, gw, n_groups, period, q_scale, emit_kv_bf16, emit_g):
    tm = x_ref.shape[0]
    chunk = ws_ref.shape[1]
    gd = gw // n_groups
    dh = fox_w // n_heads
    opt = list(opt_refs)
    kv_bf16_refs = (opt.pop(0), opt.pop(0)) if emit_kv_bf16 else (None, None)
    xb = x_ref[...].astype(BF16)
    q_ref[...] = (_dot(xb, wqkv_ref[:, 0:fox_w]) * q_scale).astype(BF16)
    for idx, (rows_ref, copy_ref) in enumerate(zip((k_ref, v_ref), kv_bf16_refs)):
        y = _dot(xb, wqkv_ref[:, (idx + 1) * fox_w:(idx + 2) * fox_w])
        for h in range(n_heads):
            rows_ref[pl.ds(h, tm, stride=n_heads), :] = y[:, h * dh:(h + 1) * dh]
        if copy_ref is not None:
            copy_ref[...] = y.astype(BF16)
    zf = _dot(xb, wf_ref[...]) + bf_ref[...]
    logf_ref[...] = _log_sigmoid(zf).T[0:n_heads, :]
    g = _layer_norm(_gelu(_dot(xb, wug_ref[:, gw:2 * gw])), lg_ref[...], lb_ref[...])
    if emit_g:
        opt.pop(0)[...] = g
    gb = g.astype(BF16)
    u = _gelu(_dot(xb, wug_ref[:, 0:gw]))
    r = lax.broadcasted_iota(jnp.int32, (chunk, chunk), 0)
    c = lax.broadcasted_iota(jnp.int32, (chunk, chunk), 1)
    sh = period.bit_length() - 1
    keep = ((r >> sh) == (c >> sh)) & ((c & (period - 1)) <= (r & (period - 1)))
    for gi in range(n_groups):
        wsg = jnp.where(keep, ws_ref[gi], 0.0).astype(BF16)
        cols = slice(gi * gd, (gi + 1) * gd)
        for ci in range(tm // chunk):
            rows = slice(ci * chunk, (ci + 1) * chunk)
            s = _dot(wsg, gb[rows, cols]) + bs_ref[gi]
            go_ref[rows, cols] = (u[rows, cols] * s).astype(BF16)


def _in_proj(x, wqkv, wf, bf, wug, lg, lb, ws, bs, *, tm, period, n_heads, q_scale, emit_kv_bf16, emit_g):
    m, d = x.shape
    fox_w = wqkv.shape[1] // 3
    gw = wug.shape[1] // 2
    n_groups = ws.shape[0]
    dh = fox_w // n_heads
    assert dh == LANES and period & (period - 1) == 0 and ws.shape[1] % period == 0
    row = lambda w: pl.BlockSpec((tm, w), lambda i: (i, 0))
    head_rows = pl.BlockSpec((tm * n_heads, dh), lambda i: (i, 0))
    out_shape = [jax.ShapeDtypeStruct((m, fox_w), BF16),
                 jax.ShapeDtypeStruct((m * n_heads, dh), F32),
                 jax.ShapeDtypeStruct((m * n_heads, dh), F32),
                 jax.ShapeDtypeStruct((n_heads, m), F32),
                 jax.ShapeDtypeStruct((m, gw), BF16)]
    out_specs = [row(fox_w), head_rows, head_rows, pl.BlockSpec((n_heads, tm), lambda i: (0, i)), row(gw)]
    if emit_kv_bf16:
        out_shape += [jax.ShapeDtypeStruct((m, fox_w), BF16)] * 2
        out_specs += [row(fox_w)] * 2
    if emit_g:
        out_shape.append(jax.ShapeDtypeStruct((m, gw), F32))
        out_specs.append(row(gw))
    kern = functools.partial(_in_proj_kernel, fox_w=fox_w, n_heads=n_heads, gw=gw, n_groups=n_groups,
                             period=period, q_scale=q_scale, emit_kv_bf16=emit_kv_bf16, emit_g=emit_g)
    return pl.pallas_call(
        kern, grid=(m // tm,),
        in_specs=[row(d), _resident(wqkv.shape), _resident(wf.shape), _resident(bf.shape),
                  _resident(wug.shape), _resident(lg.shape), _resident(lb.shape),
                  _resident(ws.shape), _resident(bs.shape)],
        out_specs=out_specs, out_shape=out_shape,
        compiler_params=_params("parallel"), name="in_proj_sgu",
    )(x, wqkv, wf, bf, wug, lg, lb, ws, bs)


def _cumsum_kernel(x_ref, o_ref, *, rows):
    x = x_ref[...]
    n = x.shape[0]
    hi = lax.Precision.HIGHEST
    ii = lax.broadcasted_iota(jnp.int32, (LANES, LANES), 0)
    jj = lax.broadcasted_iota(jnp.int32, (LANES, LANES), 1)
    within = jnp.dot(x, (ii <= jj).astype(F32), precision=hi, preferred_element_type=F32)
    totals = jnp.dot(x, jnp.ones((LANES, LANES), F32), precision=hi, preferred_element_type=F32)
    ri = lax.broadcasted_iota(jnp.int32, (n, n), 0)
    rj = lax.broadcasted_iota(jnp.int32, (n, n), 1)
    seq_i = jnp.zeros((n, n), jnp.int32)
    seq_j = jnp.zeros((n, n), jnp.int32)
    for s in range(rows, n, rows):
        seq_i += (ri >= s).astype(jnp.int32)
        seq_j += (rj >= s).astype(jnp.int32)
    earlier = ((seq_i == seq_j) & (rj < ri)).astype(F32)
    o_ref[...] = within + jnp.dot(earlier, totals, precision=hi, preferred_element_type=F32)


def _cumsum_rows(x, *, per_step):
    n, rows, _ = x.shape
    x2 = x.reshape(n * rows, LANES)
    spec = pl.BlockSpec((per_step * rows, LANES), lambda i: (i, 0))
    out = pl.pallas_call(
        functools.partial(_cumsum_kernel, rows=rows), grid=(n // per_step,),
        in_specs=[spec], out_specs=spec,
        out_shape=jax.ShapeDtypeStruct(x2.shape, F32),
        compiler_params=_params("parallel"), name="logf_cumsum",
    )(x2)
    return out.reshape(n, rows * LANES)


def _fox_prompt_kernel(q_ref, k_ref, v_ref, c_ref, mask_ref, o_ref, ka_ref, vt_ref, qa_ref, sa_ref, sb_ref,
                       pa_ref, pb_ref, aa_ref, ab_ref, m_ref, acc_ref, *, hk, hpb):
    qi = pl.program_id(2)
    tq = q_ref.shape[1]
    seq = k_ref.shape[1]
    dh = k_ref.shape[2] // hpb
    prep = FOX_PREP_ROWS
    ones_rows = (lax.broadcasted_iota(jnp.int32, (BF16_ROWS, prep), 0) == 0).astype(BF16)

    @pl.when(qi == 0)
    def _():
        row = lax.broadcasted_iota(jnp.int32, (dh, prep), 0)
        for hh in range(hpb):
            hc = slice(hh * dh, (hh + 1) * dh)
            for r0 in range(0, seq, prep):
                rs = slice(r0, r0 + prep)
                ka_ref[hh, rs, 0:dh] = k_ref[0, rs, hc]
                vt_ref[hh, 0:dh, rs] = v_ref[0, rs, hc].astype(F32).T.astype(BF16)
                vt_ref[hh, dh:dh + BF16_ROWS, rs] = ones_rows
                c2 = c_ref[0, hh:hh + 1, rs] * LOG2E
                hi = c2.astype(BF16).astype(F32)
                r1 = c2 - hi
                mid = r1.astype(BF16).astype(F32)
                lo = r1 - mid
                aug = jnp.where(row == 0, -hi, jnp.where(row == 1, -mid, jnp.where(row == 2, -lo, 0.0)))
                ka_ref[hh, rs, dh:2 * dh] = aug.T.astype(BF16)

    ones = (lax.broadcasted_iota(jnp.int32, (dh, tq), 0) < 3).astype(BF16)
    for hh in range(hpb):
        qa_ref[hh, 0:dh, :] = q_ref[0, :, hh * dh:(hh + 1) * dh].astype(F32).T.astype(BF16)
        qa_ref[hh, dh:2 * dh, :] = ones

    def qk(hh, blk):
        start = pl.multiple_of(blk * hk, hk)
        return _dot(ka_ref[hh, pl.ds(start, hk), :], qa_ref[hh])

    def softmax(hh, s, p_ref, a_ref):
        m = m_ref[hh]
        m_new = jnp.maximum(m, jnp.max(s, axis=0, keepdims=True))
        p = jnp.exp2(s - m_new)
        a = jnp.exp2(m - m_new)
        m_ref[hh] = m_new
        p_ref[hh] = p.astype(BF16)
        a_ref[hh] = a

    def pv(hh, blk, p_ref, a_ref):
        start = pl.multiple_of(blk * hk, hk)
        acc_ref[hh] = a_ref[hh] * acc_ref[hh] + _dot(vt_ref[hh, :, pl.ds(start, hk)], p_ref[hh])

    for hh in range(hpb):
        m_ref[hh] = jnp.full((1, tq), -jnp.inf, F32)
        acc_ref[hh] = jnp.zeros((dh + BF16_ROWS, tq), F32)
        pb_ref[hh] = jnp.zeros((hk, tq), BF16)
        ab_ref[hh] = jnp.ones((1, tq), F32)
        sa_ref[hh] = qk(hh, 0)

    heads = range(hpb)

    def body(j, carry):
        for hh in heads:
            sb_ref[hh] = qk(hh, 2 * j + 1)
            pv(hh, jnp.maximum(2 * j - 1, 0), pb_ref, ab_ref)
            softmax(hh, sa_ref[hh], pa_ref, aa_ref)
            sa_ref[hh] = qk(hh, 2 * j + 2)
            pv(hh, 2 * j, pa_ref, aa_ref)
            softmax(hh, sb_ref[hh], pb_ref, ab_ref)
        return carry

    lax.fori_loop(0, qi, body, 0)
    for hh in heads:
        sb_ref[hh] = qk(hh, 2 * qi + 1)
    for hh in heads:
        pv(hh, jnp.maximum(2 * qi - 1, 0), pb_ref, ab_ref)
    for hh in heads:
        softmax(hh, sa_ref[hh] + mask_ref[0], pa_ref, aa_ref)
    for hh in heads:
        pv(hh, 2 * qi, pa_ref, aa_ref)
    for hh in heads:
        softmax(hh, sb_ref[hh] + mask_ref[1], pb_ref, ab_ref)
    for hh in heads:
        pv(hh, 2 * qi + 1, pb_ref, ab_ref)
    for hh in heads:
        out = acc_ref[hh, 0:dh, :] / acc_ref[hh, dh:dh + 1, :]
        o_ref[0, :, hh * dh:(hh + 1) * dh] = out.T.astype(o_ref.dtype)


def _fox_prompt(q, k, v, c, *, n_heads, hpb):
    b, seq, width = q.shape
    dh = width // n_heads
    hk = FOX_KEY_BLOCK
    tq = 2 * hk
    groups = n_heads // hpb
    cg = c.reshape(b * groups, hpb, seq)
    wide = hpb * dh
    key = jnp.arange(2 * hk, dtype=jnp.int32).reshape(2, hk, 1)
    mask = jnp.where(key <= jnp.arange(tq, dtype=jnp.int32)[None, None, :], 0.0, -jnp.inf).astype(F32)
    return pl.pallas_call(
        functools.partial(_fox_prompt_kernel, hk=hk, hpb=hpb),
        grid=(b, groups, seq // tq),
        in_specs=[pl.BlockSpec((1, tq, wide), lambda bi, h, i: (bi, i, h)),
                  pl.BlockSpec((1, seq, wide), lambda bi, h, i: (bi, 0, h)),
                  pl.BlockSpec((1, seq, wide), lambda bi, h, i: (bi, 0, h)),
                  pl.BlockSpec((1, hpb, seq), lambda bi, h, i: (bi * groups + h, 0, 0)),
                  _resident(mask.shape)],
        out_specs=pl.BlockSpec((1, tq, wide), lambda bi, h, i: (bi, i, h)),
        out_shape=jax.ShapeDtypeStruct((b, seq, width), BF16),
        scratch_shapes=[pltpu.VMEM((hpb, seq, 2 * dh), BF16), pltpu.VMEM((hpb, dh + BF16_ROWS, seq), BF16),
                        pltpu.VMEM((hpb, 2 * dh, tq), BF16),
                        pltpu.VMEM((hpb, hk, tq), F32), pltpu.VMEM((hpb, hk, tq), F32),
                        pltpu.VMEM((hpb, hk, tq), BF16), pltpu.VMEM((hpb, hk, tq), BF16),
                        pltpu.VMEM((hpb, 1, tq), F32), pltpu.VMEM((hpb, 1, tq), F32),
                        pltpu.VMEM((hpb, 1, tq), F32),
                        pltpu.VMEM((hpb, dh + BF16_ROWS, tq), F32)],
        compiler_params=_params("parallel", "parallel", "arbitrary"), name="fox_prompt",
    )(q, k, v, cg, mask)


def _fox_sample_kernel(q_ref, kn_ref, vn_ref, kc_ref, vc_ref, c_ref, o_ref, m_ref, l_ref, acc_ref, *, n_heads):
    ci = pl.program_id(1)
    dh = kc_ref.shape[1]
    tk = kc_ref.shape[0] // n_heads
    t_new = q_ref.shape[1]
    past = pl.num_programs(1) * tk

    def update(scores, values):
        s = jnp.concatenate(scores, axis=0)
        m = m_ref[...]
        m_new = jnp.maximum(m, jnp.max(s, axis=-1, keepdims=True))
        p = jnp.exp2(s - m_new)
        a = jnp.exp2(m - m_new)
        m_ref[...] = m_new
        l_ref[...] = a * l_ref[...] + jnp.sum(p, axis=-1, keepdims=True)
        pb = p.astype(BF16)
        pv = [_dot(pb[h * t_new:(h + 1) * t_new, :], values[h]) for h in range(n_heads)]
        acc_ref[...] = a * acc_ref[...] + jnp.concatenate(pv, axis=0)

    heads = [slice(h * dh, (h + 1) * dh) for h in range(n_heads)]

    @pl.when(ci == 0)
    def _():
        m_ref[...] = jnp.full(m_ref.shape, -jnp.inf, F32)
        l_ref[...] = jnp.zeros(l_ref.shape, F32)
        acc_ref[...] = jnp.zeros(acc_ref.shape, F32)
        r = lax.broadcasted_iota(jnp.int32, (t_new, t_new), 0)
        c = lax.broadcasted_iota(jnp.int32, (t_new, t_new), 1)
        scores = []
        for h, hc in enumerate(heads):
            s = _dot_nt(q_ref[0, :, hc], kn_ref[pl.ds(h, t_new, stride=n_heads), :].astype(BF16))
            s = s - c_ref[0, h:h + 1, past:past + t_new] * LOG2E
            scores.append(jnp.where(c <= r, s, -jnp.inf))
        update(scores, [vn_ref[pl.ds(h, t_new, stride=n_heads), :].astype(BF16) for h in range(n_heads)])

    start = pl.multiple_of(ci * tk, tk)
    scores = []
    for h, hc in enumerate(heads):
        s = _dot_nt(q_ref[0, :, hc], kc_ref[pl.ds(h, tk, stride=n_heads), :].astype(BF16))
        scores.append(s - c_ref[0, h:h + 1, pl.ds(start, tk)] * LOG2E)
    update(scores, [vc_ref[pl.ds(h, tk, stride=n_heads), :].astype(BF16) for h in range(n_heads)])

    @pl.when(ci == pl.num_programs(1) - 1)
    def _():
        out = acc_ref[...] / l_ref[...]
        for h, hc in enumerate(heads):
            o_ref[0, :, hc] = out[h * t_new:(h + 1) * t_new, :].astype(o_ref.dtype)


def _fox_sample(q, kn, vn, kc, vc, c, *, tk):
    b, t_new, width = q.shape
    dh = kc.shape[1]
    n_heads = width // dh
    past = kc.shape[0] // (b * n_heads)
    chunks = past // tk
    new = pl.BlockSpec((1, t_new, width), lambda bi, ci: (bi, 0, 0))
    new_rows = pl.BlockSpec((t_new * n_heads, dh), lambda bi, ci: (bi, 0))
    old = pl.BlockSpec((tk * n_heads, dh), lambda bi, ci: (bi * chunks + ci, 0))
    return pl.pallas_call(
        functools.partial(_fox_sample_kernel, n_heads=n_heads), grid=(b, chunks),
        in_specs=[new, new_rows, new_rows, old, old,
                  pl.BlockSpec((1, n_heads, c.shape[2]), lambda bi, ci: (bi, 0, 0))],
        out_specs=new,
        out_shape=jax.ShapeDtypeStruct((b, t_new, width), BF16),
        scratch_shapes=[pltpu.VMEM((n_heads * t_new, 1), F32), pltpu.VMEM((n_heads * t_new, 1), F32),
                        pltpu.VMEM((n_heads * t_new, dh), F32)],
        compiler_params=_params("parallel", "arbitrary"), name="fox_sample",
    )(q, kn, vn, kc, vc, c)


def _out_proj_kernel(fo_ref, go_ref, x_ref, w_ref, g_ref, b_ref, o_ref, *, alpha, splits):
    half = fo_ref.shape[1]
    sub = x_ref.shape[0] // splits
    for r0 in range(0, x_ref.shape[0], sub):
        rows = slice(r0, r0 + sub)
        mix = _dot(fo_ref[rows, :], w_ref[0:half, :]) + _dot(go_ref[rows, :], w_ref[half:2 * half, :])
        o_ref[rows, :] = _layer_norm(alpha * x_ref[rows, :] + mix, g_ref[...], b_ref[...])


def _out_proj(fo, go, x, w, g, b, *, tm, alpha, splits):
    m, d = x.shape
    row = lambda wd: pl.BlockSpec((tm, wd), lambda i: (i, 0))
    return pl.pallas_call(
        functools.partial(_out_proj_kernel, alpha=alpha, splits=splits), grid=(m // tm,),
        in_specs=[row(fo.shape[1]), row(go.shape[1]), row(d),
                  _resident(w.shape), _resident(g.shape), _resident(b.shape)],
        out_specs=row(d), out_shape=jax.ShapeDtypeStruct((m, d), F32),
        compiler_params=_params("parallel"), name="out_proj_ln",
    )(fo, go, x, w, g, b)


def _mem_proj_kernel(x_ref, wk_ref, wv_ref, kt_ref, vt_ref, kb_ref, vb_ref, *, n_heads):
    tm, d = x_ref.shape
    lane_tiles = d // n_heads // LANES
    stride = lane_tiles * n_heads
    xb = x_ref[...].astype(BF16)
    for w_ref, t_ref, b_ref in ((wk_ref, kt_ref, kb_ref), (wv_ref, vt_ref, vb_ref)):
        y = _dot(xb, w_ref[...])
        b_ref[...] = y.astype(BF16)
        for hd in range(n_heads):
            for j in range(lane_tiles):
                c0 = (hd * lane_tiles + j) * LANES
                t_ref[pl.ds(j * n_heads + hd, tm, stride=stride), :] = y[:, c0:c0 + LANES]


def _mem_proj(x, wk, wv, *, tm, n_heads):
    m, d = x.shape
    rows = pl.BlockSpec((tm, d), lambda i: (i, 0))
    tiled = pl.BlockSpec((tm * d // LANES, LANES), lambda i: (i, 0))
    return pl.pallas_call(
        functools.partial(_mem_proj_kernel, n_heads=n_heads), grid=(m // tm,),
        in_specs=[rows, _resident(wk.shape), _resident(wv.shape)],
        out_specs=[tiled, tiled, rows, rows],
        out_shape=[jax.ShapeDtypeStruct((m * d // LANES, LANES), F32)] * 2
        + [jax.ShapeDtypeStruct((m, d), BF16)] * 2,
        compiler_params=_params("parallel"), name="mem_proj",
    )(x, wk, wv)


def _mem_attn_kernel(h_ref, wq_ref, wo_ref, mk_ref, mv_ref, g_ref, b_ref, o_ref, att_ref,
                     *, n_heads, n_mem, nb, alpha, scale, splits):
    tm, d = h_ref.shape
    dh = d // n_heads
    lane_tiles = dh // LANES

    def mem_head(ref, bi, hd):
        if len(ref.shape) == 3:
            return ref[bi, :, hd * dh:(hd + 1) * dh].astype(BF16)
        stride = lane_tiles * n_heads
        parts = [ref[pl.ds(bi * n_mem * stride + j * n_heads + hd, n_mem, stride=stride), :]
                 for j in range(lane_tiles)]
        return jnp.concatenate(parts, axis=1).astype(BF16)

    groups = splits if nb == 1 else 1
    rg = tm // groups
    rb = rg // nb
    pairs = [(bi, hd) for bi in range(nb) for hd in range(n_heads)]
    group_rows = [slice(gi * rg, (gi + 1) * rg) for gi in range(groups)]
    qbs = [(_dot(h_ref[rows, :].astype(BF16), wq_ref[...]) * scale).astype(BF16) for rows in group_rows]
    for gi, qb in enumerate(qbs):
        s = jnp.concatenate([_dot_nt(qb[bi * rb:(bi + 1) * rb, hd * dh:(hd + 1) * dh], mem_head(mk_ref, bi, hd))
                             for bi, hd in pairs], axis=0)
        p = jnp.exp(s - jnp.max(s, axis=-1, keepdims=True))
        l = jnp.sum(p, axis=-1, keepdims=True)
        pb = p.astype(BF16)
        for k, (bi, hd) in enumerate(pairs):
            o = _dot(pb[k * rb:(k + 1) * rb, :], mem_head(mv_ref, bi, hd)) / l[k * rb:(k + 1) * rb, :]
            att_ref[gi * rg + bi * rb:gi * rg + (bi + 1) * rb, hd * dh:(hd + 1) * dh] = o.astype(BF16)
    ys = [_dot(att_ref[rows, :], wo_ref[...]) for rows in group_rows]
    for rows, y in zip(group_rows, ys):
        o_ref[rows, :] = _layer_norm(alpha * h_ref[rows, :] + y, g_ref[...], b_ref[...])


def _mem_attn(h, wq, wo, mk, mv, g, b, *, tm, rows_per_batch, n_heads, n_mem, alpha, scale, splits):
    m, d = h.shape
    if tm <= rows_per_batch:
        nb = 1
        steps_per_batch = rows_per_batch // tm
        batch_of = lambda i: i // steps_per_batch
    else:
        nb = tm // rows_per_batch
        batch_of = lambda i: i
    row = pl.BlockSpec((tm, d), lambda i: (i, 0))
    if mk.ndim == 3:
        mem = pl.BlockSpec((nb, n_mem, d), lambda i: (batch_of(i), 0, 0))
    else:
        mem = pl.BlockSpec((nb * n_mem * d // LANES, LANES), lambda i: (batch_of(i), 0))
    return pl.pallas_call(
        functools.partial(_mem_attn_kernel, n_heads=n_heads, n_mem=n_mem, nb=nb, alpha=alpha, scale=scale,
                          splits=splits),
        grid=(m // tm,),
        in_specs=[row, _resident(wq.shape), _resident(wo.shape), mem, mem,
                  _resident(g.shape), _resident(b.shape)],
        out_specs=row, out_shape=jax.ShapeDtypeStruct((m, d), F32),
        scratch_shapes=[pltpu.VMEM((tm, d), BF16)],
        compiler_params=_params("parallel"), name="mem_attn_ln",
    )(h, wq, wo, mk, mv, g, b)


def _ffn_kernel(x_ref, wu_ref, wd_ref, g_ref, b_ref, o_ref, xb_ref, *, alpha, groups):
    f = pl.program_id(1)
    last = pl.num_programs(1) - 1
    tm, d = o_ref.shape
    slab = FFN_SLAB

    def hidden(xb):
        a = jnp.maximum(_dot(xb, wu_ref[...]), 0.0)
        return (a * a).astype(BF16)

    @pl.when(f == 0)
    def _():
        xb = x_ref[...].astype(BF16)
        xb_ref[...] = xb
        ab = hidden(xb)
        for n0 in range(0, d, slab):
            o_ref[:, n0:n0 + slab] = _dot(ab, wd_ref[:, n0:n0 + slab])

    @pl.when((f > 0) & (f < last))
    def _():
        ab = hidden(xb_ref[...])
        for n0 in range(0, d, slab):
            o_ref[:, n0:n0 + slab] += _dot(ab, wd_ref[:, n0:n0 + slab])

    @pl.when(f == last)
    def _():
        rg = tm // groups
        for r0 in range(0, tm, rg):
            rows = slice(r0, r0 + rg)
            y = o_ref[rows, :] + _dot(hidden(xb_ref[rows, :]), wd_ref[...])
            o_ref[rows, :] = _layer_norm(alpha * x_ref[rows, :] + y, g_ref[...], b_ref[...])


def _ffn(x, wu, wd, g, b, *, tm, tf, alpha, groups):
    m, d = x.shape
    dff = wu.shape[1]
    assert dff // tf >= 2
    row = pl.BlockSpec((tm, d), lambda i, f: (i, 0))
    return pl.pallas_call(
        functools.partial(_ffn_kernel, alpha=alpha, groups=groups), grid=(m // tm, dff // tf),
        in_specs=[row, pl.BlockSpec((d, tf), lambda i, f: (0, f)),
                  pl.BlockSpec((tf, d), lambda i, f: (f, 0)),
                  _resident(g.shape), _resident(b.shape)],
        out_specs=row, out_shape=jax.ShapeDtypeStruct((m, d), F32),
        scratch_shapes=[pltpu.VMEM((tm, d), BF16)],
        compiler_params=_params("parallel", "arbitrary", vmem=VMEM_LIMIT_FFN), name="ffn_ln",
    )(x, wu, wd, g, b)


def _pad_lanes(a):
    return jnp.pad(a, ((0, 0), (0, LANES - a.shape[1])))


def kernel(x_prompt, x_sample, mem_prompt, cache_fox_k, cache_fox_v, cache_fox_logf, cache_mem_k, cache_mem_v, w_in, b_f, sgu_ln_g, sgu_ln_b, w_s, b_s, w_out, ln1_g, ln1_b, w_mq, w_mk, w_mv, w_mo, ln2_g, ln2_b, w_up, w_down, ln3_g, ln3_b):
    depth = w_in.shape[0]
    assert depth == 1
    b, seq, d = x_prompt.shape
    bs_, t_new, _ = x_sample.shape
    past, n_heads, dh = cache_fox_k.shape[2:]
    n_mem, mem_heads, mem_dh = cache_mem_k.shape[2:]
    n_groups, chunk = w_s.shape[1], w_s.shape[2]
    fox_w = n_heads * dh
    gw = d - fox_w
    alpha = (2 * depth) ** 0.25
    l = 0

    wi = w_in[l]
    o3 = 3 * fox_w
    o4 = o3 + n_heads
    wqkv = wi[:, :o3].astype(BF16)
    wf = _pad_lanes(wi[:, o3:o4]).astype(BF16)
    bf = _pad_lanes(b_f[l][None, :])
    wug = wi[:, o4:].astype(BF16)
    lg, lb = sgu_ln_g[l][None, :], sgu_ln_b[l][None, :]
    ws_p, bs_p = w_s[l], b_s[l][:, :, None]
    reps = chunk // t_new
    ws_s = jnp.tile(w_s[l][:, :t_new, :t_new], (1, reps, reps))
    bs_s = jnp.tile(b_s[l][:, :t_new], (1, reps))[:, :, None]
    wo = w_out[l].astype(BF16)
    wmq, wmk, wmv, wmo = (w[l].astype(BF16) for w in (w_mq, w_mk, w_mv, w_mo))
    wu, wd = w_up[l].astype(BF16), w_down[l].astype(BF16)
    ln = [a[l][None, :] for a in (ln1_g, ln1_b, ln2_g, ln2_b, ln3_g, ln3_b)]

    def post(h, fo, go, mk, mv, *, rows_per_batch, tm_attn):
        tm_out = min(OUT_PROJ_TILE, h.shape[0])
        h = _out_proj(fo, go, h, wo, ln[0], ln[1], tm=tm_out, alpha=alpha, splits=tm_out // ROW_GROUP)
        h = _mem_attn(h, wmq, wmo, mk, mv, ln[2], ln[3], tm=tm_attn, rows_per_batch=rows_per_batch,
                      n_heads=mem_heads, n_mem=n_mem, alpha=alpha, scale=mem_dh ** -0.5,
                      splits=ROW_TILE // ROW_GROUP)
        return _ffn(h, wu, wd, ln[4], ln[5], tm=ROW_TILE, tf=FFN_CHUNK, alpha=alpha,
                    groups=ROW_TILE // ROW_GROUP)

    xp = x_prompt.reshape(b * seq, d)
    q, k, v, logf, go, kb, vb = _in_proj(xp, wqkv, wf, bf, wug, lg, lb, ws_p, bs_p, tm=ROW_TILE, period=chunk,
                                         n_heads=n_heads, q_scale=dh ** -0.5 * LOG2E,
                                         emit_kv_bf16=True, emit_g=False)
    logf = logf.reshape(n_heads, b, seq)
    lt = jnp.transpose(logf, (1, 0, 2)).reshape(b * n_heads, seq // LANES, LANES)
    c = _cumsum_rows(lt, per_step=n_heads)
    fo = _fox_prompt(q.reshape(b, seq, fox_w), kb.reshape(b, seq, fox_w), vb.reshape(b, seq, fox_w), c,
                     n_heads=n_heads, hpb=FOX_HEADS_PER_STEP)
    mem2 = mem_prompt.reshape(b * n_mem, d)
    mkt, mvt, mkb, mvb = _mem_proj(mem2, wmk, wmv, tm=ROW_TILE, n_heads=mem_heads)

    def untile_rows(a):
        a = a.reshape(b, n_mem, mem_dh // LANES, mem_heads, LANES)
        return jnp.transpose(a, (0, 1, 3, 2, 4)).reshape(1, b, n_mem, mem_heads, mem_dh)

    mk, mv = untile_rows(mkt), untile_rows(mvt)
    yp = post(xp, fo.reshape(b * seq, fox_w), go, mkb.reshape(b, n_mem, d), mvb.reshape(b, n_mem, d),
              rows_per_batch=seq, tm_attn=ROW_TILE)

    xs = x_sample.reshape(bs_ * t_new, d)
    qs, ks, vs, logfs, gos, gs = _in_proj(xs, wqkv, wf, bf, wug, lg, lb, ws_s, bs_s, tm=bs_ * t_new,
                                          period=t_new, n_heads=n_heads, q_scale=dh ** -0.5 * LOG2E,
                                          emit_kv_bf16=False, emit_g=True)
    rows_c = -(-(past + t_new) // LANES)
    rows_c = -(-rows_c // 8) * 8
    logfs = jnp.transpose(logfs.reshape(n_heads, bs_, t_new), (1, 2, 0))
    lcat = jnp.concatenate([cache_fox_logf[l], logfs], axis=1)
    lcat = jnp.pad(lcat, ((0, 0), (0, rows_c * LANES - past - t_new), (0, 0)))
    lts = jnp.transpose(lcat, (0, 2, 1)).reshape(bs_ * n_heads, rows_c, LANES)
    cs = _cumsum_rows(lts, per_step=n_heads).reshape(bs_, n_heads, rows_c * LANES)
    fos = _fox_sample(qs.reshape(bs_, t_new, fox_w), ks, vs,
                      cache_fox_k.reshape(bs_ * past * n_heads, dh), cache_fox_v.reshape(bs_ * past * n_heads, dh),
                      cs, tk=FOX_SAMPLE_KEYS)
    def tile_rows(a):
        a = a.reshape(bs_, n_mem, mem_heads, mem_dh // LANES, LANES)
        return jnp.transpose(a, (0, 1, 3, 2, 4)).reshape(bs_ * n_mem * d // LANES, LANES)

    ys = post(xs, fos.reshape(bs_ * t_new, fox_w), gos, tile_rows(cache_mem_k), tile_rows(cache_mem_v),
              rows_per_batch=t_new, tm_attn=MEM_ATTN_SAMPLE_ROWS)

    return (yp.reshape(b, seq, d), ys.reshape(bs_, t_new, d),
            k.reshape(1, b, seq, n_heads, dh), v.reshape(1, b, seq, n_heads, dh),
            jnp.transpose(logf, (1, 2, 0)).reshape(1, b, seq, n_heads),
            mk, mv,
            ks.reshape(1, bs_, t_new, n_heads, dh), vs.reshape(1, bs_, t_new, n_heads, dh),
            logfs.reshape(1, bs_, t_new, n_heads), gs.reshape(1, bs_, t_new, gw))
```
